```python
import jax, jax.numpy as jnp
from jax import lax
import numpy as np

D_MODEL = 1024
BATCH = 4
SEQ = 8192
DEPTH = 1

FOX_HEADS = 8
FOX_HEAD_DIM = D_MODEL // 16
GLA_HEADS = 4
GLA_HEAD_K = D_MODEL // 16
GLA_HEAD_V = D_MODEL // 8
GLA_GATE_RANK = 16
GLA_GATE_TAU = 16.0
GLA_CHUNK = 64
Q_BLOCK = 128
FOX_W = FOX_HEADS * FOX_HEAD_DIM
GLA_KW = GLA_HEADS * GLA_HEAD_K
GLA_VW = GLA_HEADS * GLA_HEAD_V
D_MIX = FOX_W + GLA_VW
IN_SPLITS = (FOX_W, FOX_W, FOX_W, FOX_HEADS, GLA_KW, GLA_KW, GLA_VW, GLA_VW, GLA_GATE_RANK)
D_IN = FOX_W * 3 + FOX_HEADS + GLA_KW * 2 + GLA_VW * 2 + GLA_GATE_RANK
N_MEM = 256
MEM_HEADS = 4
MEM_HEAD_DIM = D_MODEL // MEM_HEADS
D_FF = 2816
MACARON_W = 0.5
RMS_EPS = 1e-6

kernel_name = "fox_gla_hymba_macaron_sandwich_layer"


def rmsnorm(x, g):
    xf = x.astype(jnp.float32)
    y = xf * lax.rsqrt(jnp.mean(xf * xf, axis=-1, keepdims=True) + RMS_EPS)
    return (y * g.astype(jnp.float32)).astype(x.dtype)


def swiglu(x, w_gu, w_down):
    gate, up = jnp.split(x @ w_gu, 2, axis=-1)
    return (jax.nn.silu(gate) * up) @ w_down


def fox_attention(q, k, v, log_f):
    B, S, H, Dh = q.shape
    nb = S // Q_BLOCK
    c = jnp.cumsum(log_f, axis=1).transpose(0, 2, 1)
    kt = k.transpose(0, 2, 1, 3)
    vt = v.transpose(0, 2, 1, 3)
    qb = q.reshape(B, nb, Q_BLOCK, H, Dh).transpose(1, 0, 3, 2, 4)
    cb = c.reshape(B, H, nb, Q_BLOCK).transpose(2, 0, 1, 3)
    k_pos = jnp.arange(S)
    scale = Dh ** -0.5

    def block(args):
        qi, ci, i = args
        s = jnp.einsum('bhqd,bhkd->bhqk', qi, kt).astype(jnp.float32) * scale
        s = s + ci[..., None] - c[:, :, None, :]
        q_pos = i * Q_BLOCK + jnp.arange(Q_BLOCK)
        s = jnp.where(k_pos[None, :] <= q_pos[:, None], s, -jnp.inf)
        p = jax.nn.softmax(s, axis=-1).astype(vt.dtype)
        return jnp.einsum('bhqk,bhkd->bhqd', p, vt)

    o = lax.map(block, (qb, cb, jnp.arange(nb)))
    return o.transpose(1, 0, 3, 2, 4).reshape(B, S, H * Dh)


def gla_chunked(q, k, v, log_a):
    B, S, H, Dk = q.shape
    Dv = v.shape[-1]
    C = GLA_CHUNK
    nc = S // C
    f32 = jnp.float32
    qc = q.astype(f32).reshape(B, nc, C, H, Dk) * (Dk ** -0.5)
    kc = k.astype(f32).reshape(B, nc, C, H, Dk)
    vc = v.astype(f32).reshape(B, nc, C, H, Dv)
    b = jnp.cumsum(log_a.reshape(B, nc, C, H, Dk), axis=2)
    b_last = b[:, :, -1]
    q_dec = qc * jnp.exp(b)
    k_dec = kc * jnp.exp(-b)
    k_to_end = kc * jnp.exp(b_last[:, :, None] - b)
    causal = jnp.tril(jnp.ones((C, C), dtype=bool))
    a = jnp.einsum('bnihd,bnjhd->bnhij', q_dec, k_dec)
    a = jnp.where(causal, a, 0.0)
    o_intra = jnp.einsum('bnhij,bnjhv->bnihv', a, vc)
    s_chunk = jnp.einsum('bnchd,bnchv->nbhdv', k_to_end, vc)
    decay = jnp.exp(b_last).transpose(1, 0, 2, 3)

    def step(state, inp):
        d, sc = inp
        return d[..., None] * state + sc, state

    _, s_prev = lax.scan(step, jnp.zeros((B, H, Dk, Dv), f32), (decay, s_chunk))
    o_inter = jnp.einsum('bnihd,nbhdv->bnihv', q_dec, s_prev)
    return (o_intra + o_inter).reshape(B, S, H, Dv)


def hybrid_mixer(u, w_in, w_a2, b_a, b_f, g_gla, w_out):
    B, S, _ = u.shape
    offsets = [int(o) for o in np.cumsum(IN_SPLITS)[:-1]]
    fq, fk, fv, fz, gq, gk, gv, gr, ga = jnp.split(u @ w_in, offsets, axis=-1)
    log_f = jax.nn.log_sigmoid((fz + b_f).astype(jnp.float32))
    shp = (B, S, FOX_HEADS, FOX_HEAD_DIM)
    o_fox = fox_attention(fq.reshape(shp), fk.reshape(shp), fv.reshape(shp), log_f)
    log_a = jax.nn.log_sigmoid((ga @ w_a2 + b_a).astype(jnp.float32)) / GLA_GATE_TAU
    o_gla = gla_chunked(gq.reshape(B, S, GLA_HEADS, GLA_HEAD_K),
                        gk.reshape(B, S, GLA_HEADS, GLA_HEAD_K),
                        gv.reshape(B, S, GLA_HEADS, GLA_HEAD_V),
                        log_a.reshape(B, S, GLA_HEADS, GLA_HEAD_K)).astype(u.dtype)
    o_gla = rmsnorm(o_gla, g_gla.reshape(GLA_HEADS, GLA_HEAD_V)).reshape(B, S, GLA_VW)
    o_gla = o_gla * jax.nn.silu(gr)
    return jnp.concatenate([o_fox, o_gla], axis=-1) @ w_out


def memory_cross_attention(u, mem_n, w_mq, w_mkv, w_mo):
    B, S, _ = u.shape
    q = (u @ w_mq).reshape(B, S, MEM_HEADS, MEM_HEAD_DIM)
    k, v = jnp.split(mem_n @ w_mkv, 2, axis=-1)
    k = k.reshape(B, N_MEM, MEM_HEADS, MEM_HEAD_DIM)
    v = v.reshape(B, N_MEM, MEM_HEADS, MEM_HEAD_DIM)
    s = jnp.einsum('bshd,bnhd->bhsn', q, k).astype(jnp.float32) * (MEM_HEAD_DIM ** -0.5)
    p = jax.nn.softmax(s, axis=-1).astype(v.dtype)
    o = jnp.einsum('bhsn,bnhd->bshd', p, v).reshape(B, S, D_MODEL)
    return o @ w_mo


def _normal(k, shape, scale):
    return scale * jax.random.normal(k, shape, jnp.float32)


def _gain(k, shape):
    return 1.0 + 0.02 * jax.random.normal(k, shape, jnp.float32)


def setup_inputs(seed: int = 0) -> dict:
    key = jax.random.key(seed)
    ks = jax.random.split(key, 24)
    L = DEPTH
    return {
        "x": jax.random.normal(ks[0], (BATCH, SEQ, D_MODEL), jnp.float32),
        "mem": jax.random.normal(ks[1], (BATCH, N_MEM, D_MODEL), jnp.float32),
        "ffn1_g_pre": _gain(ks[2], (L, D_MODEL)),
        "ffn1_w_gu": _normal(ks[3], (L, D_MODEL, 2 * D_FF), D_MODEL ** -0.5),
        "ffn1_w_down": _normal(ks[4], (L, D_FF, D_MODEL), D_FF ** -0.5),
        "ffn1_g_post": _gain(ks[5], (L, D_MODEL)),
        "mix_g_pre": _gain(ks[6], (L, D_MODEL)),
        "mix_w_in": _normal(ks[7], (L, D_MODEL, D_IN), D_MODEL ** -0.5),
        "mix_w_a2": _normal(ks[8], (L, GLA_GATE_RANK, GLA_KW), GLA_GATE_RANK ** -0.5),
        "mix_b_a": _normal(ks[9], (L, GLA_KW), 0.1),
        "mix_b_f": _normal(ks[10], (L, FOX_HEADS), 0.1),
        "mix_g_gla": _gain(ks[11], (L, GLA_VW)),
        "mix_w_out": _normal(ks[12], (L, D_MIX, D_MODEL), D_MIX ** -0.5),
        "mix_g_post": _gain(ks[13], (L, D_MODEL)),
        "mem_g_pre": _gain(ks[14], (L, D_MODEL)),
        "mem_g_kv": _gain(ks[15], (L, D_MODEL)),
        "mem_w_q": _normal(ks[16], (L, D_MODEL, D_MODEL), D_MODEL ** -0.5),
        "mem_w_kv": _normal(ks[17], (L, D_MODEL, 2 * D_MODEL), D_MODEL ** -0.5),
        "mem_w_o": _normal(ks[18], (L, D_MODEL, D_MODEL), D_MODEL ** -0.5),
        "mem_g_post": _gain(ks[19], (L, D_MODEL)),
        "ffn2_g_pre": _gain(ks[20], (L, D_MODEL)),
        "ffn2_w_gu": _normal(ks[21], (L, D_MODEL, 2 * D_FF), D_MODEL ** -0.5),
        "ffn2_w_down": _normal(ks[22], (L, D_FF, D_MODEL), D_FF ** -0.5),
        "ffn2_g_post": _gain(ks[23], (L, D_MODEL)),
    }


def reference(x, mem, ffn1_g_pre, ffn1_w_gu, ffn1_w_down, ffn1_g_post,
              mix_g_pre, mix_w_in, mix_w_a2, mix_b_a, mix_b_f, mix_g_gla, mix_w_out, mix_g_post,
              mem_g_pre, mem_g_kv, mem_w_q, mem_w_kv, mem_w_o, mem_g_post,
              ffn2_g_pre, ffn2_w_gu, ffn2_w_down, ffn2_g_post):
    h = x
    for l in range(DEPTH):
        f = swiglu(rmsnorm(h, ffn1_g_pre[l]), ffn1_w_gu[l], ffn1_w_down[l])
        h = h + MACARON_W * rmsnorm(f, ffn1_g_post[l])
        m = hybrid_mixer(rmsnorm(h, mix_g_pre[l]), mix_w_in[l], mix_w_a2[l], mix_b_a[l],
                         mix_b_f[l], mix_g_gla[l], mix_w_out[l])
        h = h + rmsnorm(m, mix_g_post[l])
        c = memory_cross_attention(rmsnorm(h, mem_g_pre[l]), rmsnorm(mem, mem_g_kv[l]),
                                   mem_w_q[l], mem_w_kv[l], mem_w_o[l])
        h = h + rmsnorm(c, mem_g_post[l])
        f = swiglu(rmsnorm(h, ffn2_g_pre[l]), ffn2_w_gu[l], ffn2_w_down[l])
        h = h + MACARON_W * rmsnorm(f, ffn2_g_post[l])
    return h
```

```python
import functools

import numpy as np
import jax
import jax.numpy as jnp
from jax import lax
from jax.experimental import pallas as pl
from jax.experimental.pallas import tpu as pltpu

F32 = jnp.float32
BF16 = jnp.bfloat16

D_MODEL = 1024
FOX_HEADS = 8
FOX_HEAD_DIM = 64
GLA_HEADS = 4
GLA_HEAD_K = 64
GLA_HEAD_V = 128
GLA_GATE_RANK = 16
GLA_GATE_TAU = 16.0
GLA_CHUNK = 64
FOX_W = FOX_HEADS * FOX_HEAD_DIM
GLA_KW = GLA_HEADS * GLA_HEAD_K
GLA_VW = GLA_HEADS * GLA_HEAD_V
IN_SPLITS = (FOX_W, FOX_W, FOX_W, FOX_HEADS, GLA_KW, GLA_KW, GLA_VW, GLA_VW, GLA_GATE_RANK)
N_MEM = 256
MEM_HEADS = 4
MEM_HEAD_DIM = D_MODEL // MEM_HEADS
D_FF = 2816
MACARON_W = 0.5
RMS_EPS = 1e-6
LOG2E = 1.4426950408889634

LANES = 128
VMEM_LIMIT_BYTES = 56 * 1024 * 1024

TOKEN_TILE = 512
FFN_CHUNK = 256
FOX_TQ = 512
FOX_TK = 512
GLA_TILE = 512

AUG_HI, AUG_MID, AUG_LO, AUG_ONE = 0, 8, 16, 24
GA_LANE = 8


def _rms_scale(x):
    return x * lax.rsqrt(jnp.mean(x * x, axis=-1, keepdims=True) + RMS_EPS)


def _sigmoid(x):
    return 1.0 / (1.0 + jnp.exp(-x))


def _log_sigmoid(x):
    return jnp.minimum(x, 0.0) - jnp.log(1.0 + jnp.exp(-jnp.abs(x)))


def _dot(a, b):
    return jnp.dot(a, b, preferred_element_type=F32)


def _dot_nt(a, b):
    return lax.dot_general(a, b, (((1,), (1,)), ((), ())), preferred_element_type=F32)


def _dot_tn(a, b):
    return lax.dot_general(a, b, (((0,), (0,)), ((), ())), preferred_element_type=F32)


def _split3(x):
    hi = x.astype(BF16)
    r1 = x - hi.astype(F32)
    mid = r1.astype(BF16)
    lo = (r1 - mid.astype(F32)).astype(BF16)
    return hi, mid, lo


def _const_spec(shape):
    zeros = (0,) * len(shape)
    return pl.BlockSpec(shape, lambda *_: zeros, pipeline_mode=pl.Buffered(1))


def _ffn_math(h, g_pre, wgu_ref, wd_ref, g_post):
    xn = (_rms_scale(h) * g_pre).astype(BF16)
    n_chunks = D_FF // FFN_CHUNK
    acc = None
    for j in range(n_chunks):
        gu = _dot(xn, wgu_ref[:, j * 2 * FFN_CHUNK:(j + 1) * 2 * FFN_CHUNK])
        gate, up = gu[:, :FFN_CHUNK], gu[:, FFN_CHUNK:]
        act = (gate * _sigmoid(gate) * up).astype(BF16)
        part = _dot(act, wd_ref[j * FFN_CHUNK:(j + 1) * FFN_CHUNK, :])
        acc = part if acc is None else acc + part
    return h + MACARON_W * (_rms_scale(acc) * g_post)


def _ffn_kernel(h_ref, gpre_ref, wgu_ref, wd_ref, gpost_ref, o_ref):
    o_ref[...] = _ffn_math(h_ref[...], gpre_ref[...], wgu_ref, wd_ref, gpost_ref[...])


def _prep_ffn(w_gu, w_down):
    n = D_FF // FFN_CHUNK
    gate = w_gu[:, :D_FF].reshape(D_MODEL, n, FFN_CHUNK)
    up = w_gu[:, D_FF:].reshape(D_MODEL, n, FFN_CHUNK)
    wgu = jnp.concatenate([gate, up], axis=2).reshape(D_MODEL, 2 * D_FF)
    return wgu.astype(BF16), w_down.astype(BF16)


def _ffn(h, g_pre, wgu, wd, g_post):
    t = h.shape[0]
    tm = min(TOKEN_TILE, t)
    row = pl.BlockSpec((tm, D_MODEL), lambda i: (i, 0))
    return pl.pallas_call(
        _ffn_kernel,
        out_shape=jax.ShapeDtypeStruct((t, D_MODEL), F32),
        grid=(t // tm,),
        in_specs=[row, _const_spec((1, D_MODEL)), _const_spec(wgu.shape), _const_spec(wd.shape),
                  _const_spec((1, D_MODEL))],
        out_specs=row,
        compiler_params=pltpu.CompilerParams(dimension_semantics=("parallel",),
                                             vmem_limit_bytes=VMEM_LIMIT_BYTES),
        name="ffn",
    )(h, g_pre, wgu, wd, g_post)


MIX_QW = FOX_HEADS * LANES
MIX_G0 = 2 * MIX_QW
MIX_GW = 2 * GLA_KW + 2 * GLA_VW + LANES
MIX_W1 = MIX_G0 + MIX_GW


def _prep_mix(w_in, w_a2, b_a, b_f):
    offs = np.cumsum(IN_SPLITS)[:-1].tolist()
    fq, fk, fv, fz, gq, gk, gv, gr, ga = jnp.split(w_in, offs, axis=1)

    def pad_heads(w):
        w = w.reshape(D_MODEL, FOX_HEADS, FOX_HEAD_DIM)
        w = jnp.pad(w, ((0, 0), (0, 0), (0, LANES - FOX_HEAD_DIM)))
        return w.reshape(D_MODEL, MIX_QW)

    small = jnp.concatenate(
        [fz, ga, jnp.zeros((D_MODEL, LANES - FOX_HEADS - GLA_GATE_RANK), F32)], axis=1)
    w1 = jnp.concatenate([pad_heads(fq), pad_heads(fk), gq, gk, gv, gr, small], axis=1).astype(BF16)
    wvt = fv.T.astype(BF16)
    wa2 = jnp.zeros((LANES, GLA_KW), F32).at[GA_LANE:GA_LANE + GLA_GATE_RANK].set(w_a2).astype(BF16)
    bfp = jnp.zeros((1, LANES), F32).at[0, :FOX_HEADS].set(b_f)
    return w1, wvt, wa2, bfp, b_a.reshape(1, GLA_KW)


def _aug_placement():
    p = np.zeros((LANES, 2 * MIX_QW), np.float32)
    for h in range(FOX_HEADS):
        qb = h * LANES + FOX_HEAD_DIM
        kb = MIX_QW + h * LANES + FOX_HEAD_DIM
        for i, src in enumerate((AUG_HI, AUG_MID, AUG_LO)):
            p[src + h, qb + i] = 1.0
            p[AUG_ONE, kb + i] = 1.0
            p[AUG_ONE, qb + 3 + i] = 1.0
            p[src + h, kb + 3 + i] = -1.0
    return jnp.asarray(p, BF16)


def _mix_in_kernel(h_ref, g_ref, w1_ref, wvt_ref, wa2_ref, bf_ref, ba_ref, p_ref, tri_ref,
                   qaug_ref, kaug_ref, vt_ref, gq_ref, gk_ref, gv_ref, gr_ref, la_ref, carry_ref):
    @pl.when(pl.program_id(1) == 0)
    def _():
        carry_ref[...] = jnp.zeros_like(carry_ref)

    tm = h_ref.shape[0]
    xn = (_rms_scale(h_ref[...]) * g_ref[...]).astype(BF16)
    zq = _dot(xn, w1_ref[:, :MIX_QW])
    zk = _dot(xn, w1_ref[:, MIX_QW:MIX_G0])
    zg = _dot(xn, w1_ref[:, MIX_G0:])
    vt_ref[0] = _dot_nt(wvt_ref[...], xn).astype(BF16)

    o = 0
    gq_ref[...] = zg[:, o:o + GLA_KW]; o += GLA_KW
    gk_ref[...] = zg[:, o:o + GLA_KW]; o += GLA_KW
    gv_ref[...] = zg[:, o:o + GLA_VW].astype(BF16); o += GLA_VW
    gr_ref[...] = zg[:, o:o + GLA_VW]; o += GLA_VW
    small = zg[:, o:o + LANES]

    xa = _dot(small.astype(BF16), wa2_ref[...]) + ba_ref[...]
    la_ref[...] = _log_sigmoid(xa) * (1.0 / GLA_GATE_TAU)

    lane = lax.broadcasted_iota(jnp.int32, (tm, LANES), 1)
    logf = jnp.where(lane < FOX_HEADS, _log_sigmoid(small + bf_ref[...]), 0.0)
    tri = tri_ref[...]
    l_hi, l_mid, l_lo = _split3(logf)
    cum = _dot(tri, l_hi) + _dot(tri, l_mid) + _dot(tri, l_lo) + carry_ref[...]
    carry_ref[...] = cum[tm - 1:tm, :]

    c_hi, c_mid, c_lo = _split3(cum * LOG2E)
    packed = (c_hi.astype(F32) + pltpu.roll(c_mid.astype(F32), AUG_MID, 1)
              + pltpu.roll(c_lo.astype(F32), AUG_LO, 1))
    packed = jnp.where(lane == AUG_ONE, 1.0, packed)
    aug = _dot(packed.astype(BF16), p_ref[...])
    qaug_ref[...] = (zq * (FOX_HEAD_DIM ** -0.5 * LOG2E) + aug[:, :MIX_QW]).astype(BF16)
    kaug_ref[...] = (zk + aug[:, MIX_QW:]).astype(BF16)


def _mix_in(h, g, w1, wvt, wa2, bfp, ba, batch, seq):
    tm = min(TOKEN_TILE, seq)
    nt = seq // tm
    tri = jnp.asarray(np.tril(np.ones((tm, tm), np.float32)), BF16)
    row = lambda w: pl.BlockSpec((tm, w), lambda b, i: (b * nt + i, 0))
    t = batch * seq
    sds = jax.ShapeDtypeStruct
    return pl.pallas_call(
        _mix_in_kernel,
        out_shape=(sds((t, MIX_QW), BF16), sds((t, MIX_QW), BF16), sds((batch, FOX_W, seq), BF16),
                   sds((t, GLA_KW), F32), sds((t, GLA_KW), F32), sds((t, GLA_VW), BF16),
                   sds((t, GLA_VW), F32), sds((t, GLA_KW), F32)),
        grid=(batch, nt),
        in_specs=[row(D_MODEL), _const_spec((1, D_MODEL)), _const_spec(w1.shape), _const_spec(wvt.shape),
                  _const_spec(wa2.shape), _const_spec(bfp.shape), _const_spec(ba.shape),
                  _const_spec((LANES, 2 * MIX_QW)), _const_spec((tm, tm))],
        out_specs=(row(MIX_QW), row(MIX_QW), pl.BlockSpec((1, FOX_W, tm), lambda b, i: (b, 0, i)),
                   row(GLA_KW), row(GLA_KW), row(GLA_VW), row(GLA_VW), row(GLA_KW)),
        scratch_shapes=[pltpu.VMEM((1, LANES), F32)],
        compiler_params=pltpu.CompilerParams(dimension_semantics=("arbitrary", "arbitrary"),
                                             vmem_limit_bytes=VMEM_LIMIT_BYTES),
        name="mix_in",
    )(h, g, w1, wvt, wa2, bfp, ba, _aug_placement(), tri)


def _fox_kernel(q_ref, k_ref, vt_ref, o_ref, *, tq, tk):
    qi = pl.program_id(2)
    q = q_ref[...]
    sub = tq // tk

    def step(j, carry, masked):
        m, l, acc = carry
        ks = pl.multiple_of(j * tk, tk)
        s = _dot_nt(k_ref[pl.ds(ks, tk), :], q)
        if masked:
            kpos = ks + lax.broadcasted_iota(jnp.int32, (tk, tq), 0)
            qpos = qi * tq + lax.broadcasted_iota(jnp.int32, (tk, tq), 1)
            s = jnp.where(kpos <= qpos, s, -jnp.inf)
        m_new = jnp.maximum(m, jnp.max(s, axis=0, keepdims=True))
        p = jnp.exp2(s - m_new)
        alpha = jnp.exp2(m - m_new)
        l = alpha * l + jnp.sum(p, axis=0, keepdims=True)
        acc = alpha * acc + _dot(vt_ref[0, :, pl.ds(ks, tk)], p.astype(BF16))
        return m_new, l, acc

    init = (jnp.full((1, tq), -jnp.inf, F32), jnp.zeros((1, tq), F32),
            jnp.zeros((FOX_HEAD_DIM, tq), F32))
    carry = lax.fori_loop(0, qi * sub, lambda j, c: step(j, c, False), init)
    for d in range(sub):
        carry = step(qi * sub + d, carry, True)
    _, l, acc = carry
    o_ref[0] = (acc / l).astype(BF16)


def _fox(qaug, kaug, vt, batch, seq):
    tq = min(FOX_TQ, seq)
    tk = min(FOX_TK, tq)
    nq = seq // tq
    return pl.pallas_call(
        functools.partial(_fox_kernel, tq=tq, tk=tk),
        out_shape=jax.ShapeDtypeStruct((batch, FOX_W, seq), BF16),
        grid=(batch, FOX_HEADS, nq),
        in_specs=[pl.BlockSpec((tq, LANES), lambda b, h, i: (b * nq + i, h)),
                  pl.BlockSpec((seq, LANES), lambda b, h, i: (b, h)),
                  pl.BlockSpec((1, FOX_HEAD_DIM, seq), lambda b, h, i: (b, h, 0))],
        out_specs=pl.BlockSpec((1, FOX_HEAD_DIM, tq), lambda b, h, i: (b, h, i)),
        compiler_params=pltpu.CompilerParams(dimension_semantics=("parallel", "parallel", "arbitrary"),
                                             vmem_limit_bytes=VMEM_LIMIT_BYTES),
        name="fox",
    )(qaug, kaug, vt)


def _gla_kernel(gq_ref, gk_ref, gv_ref, gr_ref, la_ref, g_ref, tri_ref, o_ref, st_ref, *, nb, tile):
    @pl.when(pl.program_id(0) == 0)
    def _():
        st_ref[...] = jnp.zeros_like(st_ref)

    c = GLA_CHUNK
    lane_head = lax.broadcasted_iota(jnp.int32, (c, GLA_KW), 1) // GLA_HEAD_K
    st_head = lax.broadcasted_iota(jnp.int32, (GLA_HEAD_V, GLA_KW), 1) // GLA_HEAD_K
    causal = (lax.broadcasted_iota(jnp.int32, (c, c), 0) >= lax.broadcasted_iota(jnp.int32, (c, c), 1))
    tri = tri_ref[...]

    def chunk(ci, _):
        rows = pl.ds(pl.multiple_of(ci * c, c), c)
        for b in range(nb):
            la = la_ref[b, rows, :]
            la_hi = la.astype(BF16)
            la_lo = (la - la_hi.astype(F32)).astype(BF16)
            bc = _dot(tri, la_hi) + _dot(tri, la_lo)
            bl = bc[c - 1:c, :]
            k = gk_ref[b, rows, :]
            qd = gq_ref[b, rows, :] * (GLA_HEAD_K ** -0.5) * jnp.exp(bc)
            kd = (k * jnp.exp(-bc)).astype(BF16)
            ke = (k * jnp.exp(bl - bc)).astype(BF16)
            st = st_ref[b]
            st_lo = st.astype(BF16)
            upd = jnp.zeros_like(st)
            for h in range(GLA_HEADS):
                vcols = slice(h * GLA_HEAD_V, (h + 1) * GLA_HEAD_V)
                qdh = jnp.where(lane_head == h, qd, 0.0).astype(BF16)
                a = jnp.where(causal, _dot_nt(qdh, kd), 0.0)
                vh = gv_ref[b, rows, vcols]
                o = _dot(a.astype(BF16), vh) + _dot_nt(qdh, st_lo)
                gr = gr_ref[b, rows, vcols]
                o = _rms_scale(o) * g_ref[:, vcols] * (gr * _sigmoid(gr))
                o_ref[b, rows, vcols] = o.astype(BF16)
                upd = upd + jnp.where(st_head == h, _dot_tn(vh, ke), 0.0)
            st_ref[b] = st * jnp.exp(bl) + upd
        return 0

    lax.fori_loop(0, tile // c, chunk, 0)


def _gla(gq, gk, gv, gr, la, g_gla, batch, seq):
    tile = min(GLA_TILE, seq)
    tri = jnp.asarray(np.tril(np.ones((GLA_CHUNK, GLA_CHUNK), np.float32)), BF16)
    blk = lambda w: pl.BlockSpec((batch, tile, w), lambda i: (0, i, 0))
    r3 = lambda x: x.reshape(batch, seq, x.shape[-1])
    out = pl.pallas_call(
        functools.partial(_gla_kernel, nb=batch, tile=tile),
        out_shape=jax.ShapeDtypeStruct((batch, seq, GLA_VW), BF16),
        grid=(seq // tile,),
        in_specs=[blk(GLA_KW), blk(GLA_KW), blk(GLA_VW), blk(GLA_VW), blk(GLA_KW),
                  _const_spec((1, GLA_VW)), _const_spec((GLA_CHUNK, GLA_CHUNK))],
        out_specs=blk(GLA_VW),
        scratch_shapes=[pltpu.VMEM((batch, GLA_HEAD_V, GLA_KW), F32)],
        compiler_params=pltpu.CompilerParams(dimension_semantics=("arbitrary",),
                                             vmem_limit_bytes=VMEM_LIMIT_BYTES),
        name="gla",
    )(r3(gq), r3(gk), r3(gv), r3(gr), r3(la), g_gla, tri)
    return out.reshape(batch * seq, GLA_VW)


def _memkv_kernel(mem_ref, g_ref, w_ref, k_ref, v_ref):
    mn = (_rms_scale(mem_ref[0]) * g_ref[...]).astype(BF16)
    kv = _dot(mn, w_ref[...])
    k_ref[0] = kv[:, :D_MODEL].astype(BF16)
    v_ref[0] = kv[:, D_MODEL:].astype(BF16)


def _memkv(mem, g, w):
    batch = mem.shape[0]
    blk = pl.BlockSpec((1, N_MEM, D_MODEL), lambda b: (b, 0, 0))
    sds = jax.ShapeDtypeStruct((batch, N_MEM, D_MODEL), BF16)
    return pl.pallas_call(
        _memkv_kernel,
        out_shape=(sds, sds),
        grid=(batch,),
        in_specs=[blk, _const_spec((1, D_MODEL)), _const_spec(w.shape)],
        out_specs=(blk, blk),
        compiler_params=pltpu.CompilerParams(dimension_semantics=("parallel",),
                                             vmem_limit_bytes=VMEM_LIMIT_BYTES),
        name="memkv",
    )(mem, g, w)


def _post_kernel(h_ref, oft_ref, og_ref, wof_ref, wog_ref, g1_ref, g2_ref, wq_ref, km_ref, vm_ref,
                 wo_ref, g3_ref, out_ref):
    m = _dot_tn(oft_ref[0], wof_ref[...]) + _dot(og_ref[...], wog_ref[...])
    h2 = h_ref[...] + _rms_scale(m) * g1_ref[...]
    u = (_rms_scale(h2) * g2_ref[...]).astype(BF16)
    q = _dot(u, wq_ref[...]) * (MEM_HEAD_DIM ** -0.5)
    heads = []
    for hd in range(MEM_HEADS):
        cols = slice(hd * MEM_HEAD_DIM, (hd + 1) * MEM_HEAD_DIM)
        s = _dot_nt(q[:, cols].astype(BF16), km_ref[0, :, cols])
        p = jnp.exp(s - jnp.max(s, axis=-1, keepdims=True))
        p = p / jnp.sum(p, axis=-1, keepdims=True)
        heads.append(_dot(p.astype(BF16), vm_ref[0, :, cols]))
    o = jnp.concatenate(heads, axis=1).astype(BF16)
    c = _dot(o, wo_ref[...])
    out_ref[...] = h2 + _rms_scale(c) * g3_ref[...]


def _post(h, oft, og, wof, wog, g1, g2, wq, km, vm, wo, g3, batch, seq):
    tm = min(TOKEN_TILE, seq)
    nt = seq // tm
    row = lambda w: pl.BlockSpec((tm, w), lambda b, i: (b * nt + i, 0))
    memblk = pl.BlockSpec((1, N_MEM, D_MODEL), lambda b, i: (b, 0, 0))
    g = _const_spec((1, D_MODEL))
    return pl.pallas_call(
        _post_kernel,
        out_shape=jax.ShapeDtypeStruct((batch * seq, D_MODEL), F32),
        grid=(batch, nt),
        in_specs=[row(D_MODEL), pl.BlockSpec((1, FOX_W, tm), lambda b, i: (b, 0, i)), row(GLA_VW),
                  _const_spec(wof.shape), _const_spec(wog.shape), g, g, _const_spec(wq.shape),
                  memblk, memblk, _const_spec(wo.shape), g],
        out_specs=row(D_MODEL),
        compiler_params=pltpu.CompilerParams(dimension_semantics=("parallel", "parallel"),
                                             vmem_limit_bytes=VMEM_LIMIT_BYTES),
        name="post",
    )(h, oft, og, wof, wog, g1, g2, wq, km, vm, wo, g3)


def kernel(x, mem, ffn1_g_pre, ffn1_w_gu, ffn1_w_down, ffn1_g_post, mix_g_pre, mix_w_in, mix_w_a2, mix_b_a,
           mix_b_f, mix_g_gla, mix_w_out, mix_g_post, mem_g_pre, mem_g_kv, mem_w_q, mem_w_kv, mem_w_o,
           mem_g_post, ffn2_g_pre, ffn2_w_gu, ffn2_w_down, ffn2_g_post):
    batch, seq, _ = x.shape
    depth = ffn1_g_pre.shape[0]
    row = lambda v: v.reshape(1, -1)
    h = x.reshape(batch * seq, D_MODEL)
    for l in range(depth):
        wgu1, wd1 = _prep_ffn(ffn1_w_gu[l], ffn1_w_down[l])
        wgu2, wd2 = _prep_ffn(ffn2_w_gu[l], ffn2_w_down[l])
        w1, wvt, wa2, bfp, ba = _prep_mix(mix_w_in[l], mix_w_a2[l], mix_b_a[l], mix_b_f[l])
        w_out = mix_w_out[l].astype(BF16)

        h = _ffn(h, row(ffn1_g_pre[l]), wgu1, wd1, row(ffn1_g_post[l]))
        qaug, kaug, vt, gq, gk, gv, gr, la = _mix_in(h, row(mix_g_pre[l]), w1, wvt, wa2, bfp, ba, batch, seq)
        oft = _fox(qaug, kaug, vt, batch, seq)
        og = _gla(gq, gk, gv, gr, la, row(mix_g_gla[l]), batch, seq)
        km, vm = _memkv(mem, row(mem_g_kv[l]), mem_w_kv[l].astype(BF16))
        h = _post(h, oft, og, w_out[:FOX_W], w_out[FOX_W:], row(mix_g_post[l]), row(mem_g_pre[l]),
                  mem_w_q[l].astype(BF16), km, vm, mem_w_o[l].astype(BF16), row(mem_g_post[l]), batch, seq)
        h = _ffn(h, row(ffn2_g_pre[l]), wgu2, wd2, row(ffn2_g_post[l]))
    return h.reshape(batch, seq, D_MODEL)
```

```python
import functools

import numpy as np
import jax
import jax.numpy as jnp
from jax import lax
from jax.experimental import pallas as pl
from jax.experimental.pallas import tpu as pltpu

F32 = jnp.float32
BF16 = jnp.bfloat16

D_MODEL = 1024
FOX_HEADS = 8
FOX_HEAD_DIM = 64
GLA_HEADS = 4
GLA_HEAD_K = 64
GLA_HEAD_V = 128
GLA_GATE_RANK = 16
GLA_GATE_TAU = 16.0
GLA_CHUNK = 64
FOX_W = FOX_HEADS * FOX_HEAD_DIM
GLA_KW = GLA_HEADS * GLA_HEAD_K
GLA_VW = GLA_HEADS * GLA_HEAD_V
IN_SPLITS = (FOX_W, FOX_W, FOX_W, FOX_HEADS, GLA_KW, GLA_KW, GLA_VW, GLA_VW, GLA_GATE_RANK)
N_MEM = 256
MEM_HEADS = 4
MEM_HEAD_DIM = D_MODEL // MEM_HEADS
D_FF = 2816
MACARON_W = 0.5
RMS_EPS = 1e-6
LOG2E = 1.4426950408889634

LANES = 128
VMEM_LIMIT_BYTES = 56 * 1024 * 1024

TOKEN_TILE = 512
FFN_CHUNK = 256
FOX_TQ = 512
FOX_TK = 512
FOX_HEADS_PER_STEP = 4
FOX_LAZY_HEADROOM = 60.0
GLA_TILE = 512

AUG_HI, AUG_MID, AUG_LO, AUG_ONE = 0, 8, 16, 24
GA_LANE = 8


def _rms_scale(x):
    return x * lax.rsqrt(jnp.mean(x * x, axis=-1, keepdims=True) + RMS_EPS)


def _sigmoid(x):
    return 1.0 / (1.0 + jnp.exp(-x))


def _log_sigmoid(x):
    return jnp.minimum(x, 0.0) - jnp.log(1.0 + jnp.exp(-jnp.abs(x)))


def _dot(a, b):
    return jnp.dot(a, b, preferred_element_type=F32)


def _dot_nt(a, b):
    return lax.dot_general(a, b, (((1,), (1,)), ((), ())), preferred_element_type=F32)


def _dot_tn(a, b):
    return lax.dot_general(a, b, (((0,), (0,)), ((), ())), preferred_element_type=F32)


def _split3(x):
    hi = x.astype(BF16)
    r1 = x - hi.astype(F32)
    mid = r1.astype(BF16)
    lo = (r1 - mid.astype(F32)).astype(BF16)
    return hi, mid, lo


def _const_spec(shape):
    zeros = (0,) * len(shape)
    return pl.BlockSpec(shape, lambda *_: zeros, pipeline_mode=pl.Buffered(1))


def _ffn_math(h, g_pre, wgu_ref, wd_ref, g_post):
    xn = (_rms_scale(h) * g_pre).astype(BF16)
    n_chunks = D_FF // FFN_CHUNK
    acc = None
    for j in range(n_chunks):
        gu = _dot(xn, wgu_ref[:, j * 2 * FFN_CHUNK:(j + 1) * 2 * FFN_CHUNK])
        gate, up = gu[:, :FFN_CHUNK], gu[:, FFN_CHUNK:]
        act = (gate * _sigmoid(gate) * up).astype(BF16)
        part = _dot(act, wd_ref[j * FFN_CHUNK:(j + 1) * FFN_CHUNK, :])
        acc = part if acc is None else acc + part
    return h + MACARON_W * (_rms_scale(acc) * g_post)


def _ffn_kernel(h_ref, gpre_ref, wgu_ref, wd_ref, gpost_ref, o_ref):
    o_ref[...] = _ffn_math(h_ref[...], gpre_ref[...], wgu_ref, wd_ref, gpost_ref[...])


def _prep_ffn(w_gu, w_down):
    n = D_FF // FFN_CHUNK
    gate = w_gu[:, :D_FF].reshape(D_MODEL, n, FFN_CHUNK)
    up = w_gu[:, D_FF:].reshape(D_MODEL, n, FFN_CHUNK)
    wgu = jnp.concatenate([gate, up], axis=2).reshape(D_MODEL, 2 * D_FF)
    return wgu.astype(BF16), w_down.astype(BF16)


def _ffn(h, g_pre, wgu, wd, g_post):
    t = h.shape[0]
    tm = min(TOKEN_TILE, t)
    row = pl.BlockSpec((tm, D_MODEL), lambda i: (i, 0))
    return pl.pallas_call(
        _ffn_kernel,
        out_shape=jax.ShapeDtypeStruct((t, D_MODEL), F32),
        grid=(t // tm,),
        in_specs=[row, _const_spec((1, D_MODEL)), _const_spec(wgu.shape), _const_spec(wd.shape),
                  _const_spec((1, D_MODEL))],
        out_specs=row,
        compiler_params=pltpu.CompilerParams(dimension_semantics=("parallel",),
                                             vmem_limit_bytes=VMEM_LIMIT_BYTES),
        name="ffn",
    )(h, g_pre, wgu, wd, g_post)


MIX_QW = FOX_HEADS * LANES
MIX_G0 = 2 * MIX_QW
MIX_GW = 2 * GLA_KW + 2 * GLA_VW + LANES
MIX_W1 = MIX_G0 + MIX_GW


def _prep_mix(w_in, w_a2, b_a, b_f):
    offs = np.cumsum(IN_SPLITS)[:-1].tolist()
    fq, fk, fv, fz, gq, gk, gv, gr, ga = jnp.split(w_in, offs, axis=1)

    def pad_heads(w):
        w = w.reshape(D_MODEL, FOX_HEADS, FOX_HEAD_DIM)
        w = jnp.pad(w, ((0, 0), (0, 0), (0, LANES - FOX_HEAD_DIM)))
        return w.reshape(D_MODEL, MIX_QW)

    small = jnp.concatenate(
        [fz, ga, jnp.zeros((D_MODEL, LANES - FOX_HEADS - GLA_GATE_RANK), F32)], axis=1)
    w1 = jnp.concatenate([pad_heads(fq), pad_heads(fk), gq, gk, gv, gr, small], axis=1).astype(BF16)
    wvt = fv.T.astype(BF16)
    wa2 = jnp.zeros((LANES, GLA_KW), F32).at[GA_LANE:GA_LANE + GLA_GATE_RANK].set(w_a2).astype(BF16)
    bfp = jnp.zeros((1, LANES), F32).at[0, :FOX_HEADS].set(b_f)
    return w1, wvt, wa2, bfp, b_a.reshape(1, GLA_KW)


def _aug_placement():
    p = np.zeros((LANES, 2 * MIX_QW), np.float32)
    for h in range(FOX_HEADS):
        qb = h * LANES + FOX_HEAD_DIM
        kb = MIX_QW + h * LANES + FOX_HEAD_DIM
        for i, src in enumerate((AUG_HI, AUG_MID, AUG_LO)):
            p[src + h, qb + i] = 1.0
            p[AUG_ONE, kb + i] = 1.0
            p[AUG_ONE, qb + 3 + i] = 1.0
            p[src + h, kb + 3 + i] = -1.0
    return jnp.asarray(p, BF16)


def _mix_in_kernel(h_ref, g_ref, w1_ref, wvt_ref, wa2_ref, bf_ref, ba_ref, p_ref, tri_ref,
                   qaug_ref, kaug_ref, vt_ref, gq_ref, gk_ref, gv_ref, gr_ref, la_ref, carry_ref):
    @pl.when(pl.program_id(1) == 0)
    def _():
        carry_ref[...] = jnp.zeros_like(carry_ref)

    tm = h_ref.shape[0]
    xn = (_rms_scale(h_ref[...]) * g_ref[...]).astype(BF16)
    zq = _dot(xn, w1_ref[:, :MIX_QW])
    zk = _dot(xn, w1_ref[:, MIX_QW:MIX_G0])
    zg = _dot(xn, w1_ref[:, MIX_G0:])
    vt_ref[0] = _dot_nt(wvt_ref[...], xn).astype(BF16)

    o = 0
    gq_ref[...] = zg[:, o:o + GLA_KW]; o += GLA_KW
    gk_ref[...] = zg[:, o:o + GLA_KW]; o += GLA_KW
    gv_ref[...] = zg[:, o:o + GLA_VW].astype(BF16); o += GLA_VW
    gr_ref[...] = zg[:, o:o + GLA_VW]; o += GLA_VW
    small = zg[:, o:o + LANES]

    xa = _dot(small.astype(BF16), wa2_ref[...]) + ba_ref[...]
    la_ref[...] = _log_sigmoid(xa) * (1.0 / GLA_GATE_TAU)

    lane = lax.broadcasted_iota(jnp.int32, (tm, LANES), 1)
    logf = jnp.where(lane < FOX_HEADS, _log_sigmoid(small + bf_ref[...]), 0.0)
    tri = tri_ref[...]
    l_hi, l_mid, l_lo = _split3(logf)
    cum = _dot(tri, l_hi) + _dot(tri, l_mid) + _dot(tri, l_lo) + carry_ref[...]
    carry_ref[...] = cum[tm - 1:tm, :]

    c_hi, c_mid, c_lo = _split3(cum * LOG2E)
    packed = (c_hi.astype(F32) + pltpu.roll(c_mid.astype(F32), AUG_MID, 1)
              + pltpu.roll(c_lo.astype(F32), AUG_LO, 1))
    packed = jnp.where(lane == AUG_ONE, 1.0, packed)
    aug = _dot(packed.astype(BF16), p_ref[...])
    qaug_ref[...] = (zq * (FOX_HEAD_DIM ** -0.5 * LOG2E) + aug[:, :MIX_QW]).astype(BF16)
    kaug_ref[...] = (zk + aug[:, MIX_QW:]).astype(BF16)


def _mix_in(h, g, w1, wvt, wa2, bfp, ba, batch, seq):
    tm = min(TOKEN_TILE, seq)
    nt = seq // tm
    tri = jnp.asarray(np.tril(np.ones((tm, tm), np.float32)), BF16)
    row = lambda w: pl.BlockSpec((tm, w), lambda b, i: (b * nt + i, 0))
    t = batch * seq
    sds = jax.ShapeDtypeStruct
    return pl.pallas_call(
        _mix_in_kernel,
        out_shape=(sds((t, MIX_QW), BF16), sds((t, MIX_QW), BF16), sds((batch, FOX_W, seq), BF16),
                   sds((t, GLA_KW), F32), sds((t, GLA_KW), F32), sds((t, GLA_VW), BF16),
                   sds((t, GLA_VW), F32), sds((t, GLA_KW), F32)),
        grid=(batch, nt),
        in_specs=[row(D_MODEL), _const_spec((1, D_MODEL)), _const_spec(w1.shape), _const_spec(wvt.shape),
                  _const_spec(wa2.shape), _const_spec(bfp.shape), _const_spec(ba.shape),
                  _const_spec((LANES, 2 * MIX_QW)), _const_spec((tm, tm))],
        out_specs=(row(MIX_QW), row(MIX_QW), pl.BlockSpec((1, FOX_W, tm), lambda b, i: (b, 0, i)),
                   row(GLA_KW), row(GLA_KW), row(GLA_VW), row(GLA_VW), row(GLA_KW)),
        scratch_shapes=[pltpu.VMEM((1, LANES), F32)],
        compiler_params=pltpu.CompilerParams(dimension_semantics=("arbitrary", "arbitrary"),
                                             vmem_limit_bytes=VMEM_LIMIT_BYTES),
        name="mix_in",
    )(h, g, w1, wvt, wa2, bfp, ba, _aug_placement(), tri)


def _fox_kernel(q_ref, k_ref, vt_ref, o_ref, *, tq, tk, heads):
    qi = pl.program_id(2)
    dh = FOX_HEAD_DIM
    assert tq == tk

    def scores(hh, ks):
        cols = slice(hh * LANES, (hh + 1) * LANES)
        return _dot_nt(k_ref[pl.ds(ks, tk), cols], q_ref[:, cols])

    def weighted_values(hh, ks, p):
        return _dot(vt_ref[0, hh * dh:(hh + 1) * dh, pl.ds(ks, tk)], p.astype(BF16))

    def exact_step(hh, j, carry, masked):
        m, l, acc = carry
        ks = pl.multiple_of(j * tk, tk)
        s = scores(hh, ks)
        if masked:
            kpos = ks + lax.broadcasted_iota(jnp.int32, (tk, tq), 0)
            qpos = qi * tq + lax.broadcasted_iota(jnp.int32, (tk, tq), 1)
            s = jnp.where(kpos <= qpos, s, -jnp.inf)
        m_new = jnp.maximum(m, jnp.max(s, axis=0, keepdims=True))
        p = jnp.exp2(s - m_new)
        alpha = jnp.exp2(m - m_new)
        l = alpha * l + jnp.sum(p, axis=0, keepdims=True)
        acc = alpha * acc + weighted_values(hh, ks, p)
        return m_new, l, acc

    def lazy_update(hh, ks, s, carry):
        m, l, acc, over = carry
        p = jnp.exp2(s - m)
        mb = jnp.max(s, axis=0, keepdims=True)
        l = l + jnp.sum(p, axis=0, keepdims=True)
        acc = acc + weighted_values(hh, ks, p)
        m_new = jnp.maximum(m, mb)
        alpha = jnp.exp2(m - m_new)
        return m_new, l * alpha, acc * alpha, jnp.maximum(over, mb - m)

    def lazy_block(j, carry):
        ks = pl.multiple_of(j * tk, tk)
        out = [None] * heads
        s = [scores(0, ks)] + [None] * (heads - 1)
        for hh in range(heads):
            if hh + 1 < heads:
                s[hh + 1] = scores(hh + 1, ks)
            out[hh] = lazy_update(hh, ks, s[hh], carry[hh])
        return tuple(out)

    def finish(hh, l, acc):
        o_ref[0, hh * dh:(hh + 1) * dh, :] = (acc / l).astype(BF16)

    init = (jnp.full((1, tq), -jnp.inf, F32), jnp.zeros((1, tq), F32), jnp.zeros((dh, tq), F32))
    diag = tuple(exact_step(hh, qi, init, True) for hh in range(heads))
    carry = tuple(c + (jnp.zeros((1, tq), F32),) for c in diag)
    carry = lax.fori_loop(0, qi, lazy_block, carry)
    worst = jnp.max(jnp.concatenate([c[3] for c in carry], axis=0))
    for hh in range(heads):
        finish(hh, carry[hh][1], carry[hh][2])

    @pl.when(jnp.logical_not(worst <= FOX_LAZY_HEADROOM))
    def _():
        for hh in range(heads):
            c = lax.fori_loop(0, qi, lambda j, c, hh=hh: exact_step(hh, j, c, False), diag[hh])
            finish(hh, c[1], c[2])


def _fox(qaug, kaug, vt, batch, seq):
    tq = min(FOX_TQ, seq)
    tk = min(FOX_TK, tq)
    nq = seq // tq
    hp = FOX_HEADS_PER_STEP
    return pl.pallas_call(
        functools.partial(_fox_kernel, tq=tq, tk=tk, heads=hp),
        out_shape=jax.ShapeDtypeStruct((batch, FOX_W, seq), BF16),
        grid=(batch, FOX_HEADS // hp, nq),
        in_specs=[pl.BlockSpec((tq, hp * LANES), lambda b, h, i: (b * nq + i, h)),
                  pl.BlockSpec((seq, hp * LANES), lambda b, h, i: (b, h), pipeline_mode=pl.Buffered(1)),
                  pl.BlockSpec((1, hp * FOX_HEAD_DIM, seq), lambda b, h, i: (b, h, 0),
                               pipeline_mode=pl.Buffered(1))],
        out_specs=pl.BlockSpec((1, hp * FOX_HEAD_DIM, tq), lambda b, h, i: (b, h, i)),
        compiler_params=pltpu.CompilerParams(dimension_semantics=("parallel", "parallel", "arbitrary"),
                                             vmem_limit_bytes=VMEM_LIMIT_BYTES),
        name="fox",
    )(qaug, kaug, vt)


def _gla_kernel(gq_ref, gk_ref, gv_ref, gr_ref, la_ref, g_ref, tri_ref, o_ref, st_ref, *, nb, tile):
    @pl.when(pl.program_id(0) == 0)
    def _():
        st_ref[...] = jnp.zeros_like(st_ref)

    c = GLA_CHUNK
    lane_head = lax.broadcasted_iota(jnp.int32, (c, GLA_KW), 1) // GLA_HEAD_K
    st_head = lax.broadcasted_iota(jnp.int32, (GLA_HEAD_V, GLA_KW), 1) // GLA_HEAD_K
    causal = (lax.broadcasted_iota(jnp.int32, (c, c), 0) >= lax.broadcasted_iota(jnp.int32, (c, c), 1))
    tri = tri_ref[...]

    def chunk(ci, _):
        rows = pl.ds(pl.multiple_of(ci * c, c), c)
        for b in range(nb):
            la = la_ref[b, rows, :]
            la_hi = la.astype(BF16)
            la_lo = (la - la_hi.astype(F32)).astype(BF16)
            bc = _dot(tri, la_hi) + _dot(tri, la_lo)
            bl = bc[c - 1:c, :]
            k = gk_ref[b, rows, :]
            qd = gq_ref[b, rows, :] * (GLA_HEAD_K ** -0.5) * jnp.exp(bc)
            kd = (k * jnp.exp(-bc)).astype(BF16)
            ke = (k * jnp.exp(bl - bc)).astype(BF16)
            st = st_ref[b]
            st_lo = st.astype(BF16)
            upd = jnp.zeros_like(st)
            for h in range(GLA_HEADS):
                vcols = slice(h * GLA_HEAD_V, (h + 1) * GLA_HEAD_V)
                qdh = jnp.where(lane_head == h, qd, 0.0).astype(BF16)
                a = jnp.where(causal, _dot_nt(qdh, kd), 0.0)
                vh = gv_ref[b, rows, vcols]
                o = _dot(a.astype(BF16), vh) + _dot_nt(qdh, st_lo)
                gr = gr_ref[b, rows, vcols]
                o = _rms_scale(o) * g_ref[:, vcols] * (gr * _sigmoid(gr))
                o_ref[b, rows, vcols] = o.astype(BF16)
                upd = upd + jnp.where(st_head == h, _dot_tn(vh, ke), 0.0)
            st_ref[b] = st * jnp.exp(bl) + upd
        return 0

    lax.fori_loop(0, tile // c, chunk, 0)


def _gla(gq, gk, gv, gr, la, g_gla, batch, seq):
    tile = min(GLA_TILE, seq)
    tri = jnp.asarray(np.tril(np.ones((GLA_CHUNK, GLA_CHUNK), np.float32)), BF16)
    blk = lambda w: pl.BlockSpec((batch, tile, w), lambda i: (0, i, 0))
    r3 = lambda x: x.reshape(batch, seq, x.shape[-1])
    out = pl.pallas_call(
        functools.partial(_gla_kernel, nb=batch, tile=tile),
        out_shape=jax.ShapeDtypeStruct((batch, seq, GLA_VW), BF16),
        grid=(seq // tile,),
        in_specs=[blk(GLA_KW), blk(GLA_KW), blk(GLA_VW), blk(GLA_VW), blk(GLA_KW),
                  _const_spec((1, GLA_VW)), _const_spec((GLA_CHUNK, GLA_CHUNK))],
        out_specs=blk(GLA_VW),
        scratch_shapes=[pltpu.VMEM((batch, GLA_HEAD_V, GLA_KW), F32)],
        compiler_params=pltpu.CompilerParams(dimension_semantics=("arbitrary",),
                                             vmem_limit_bytes=VMEM_LIMIT_BYTES),
        name="gla",
    )(r3(gq), r3(gk), r3(gv), r3(gr), r3(la), g_gla, tri)
    return out.reshape(batch * seq, GLA_VW)


def _memkv_kernel(mem_ref, g_ref, w_ref, k_ref, v_ref):
    mn = (_rms_scale(mem_ref[0]) * g_ref[...]).astype(BF16)
    kv = _dot(mn, w_ref[...])
    k_ref[0] = kv[:, :D_MODEL].astype(BF16)
    v_ref[0] = kv[:, D_MODEL:].astype(BF16)


def _memkv(mem, g, w):
    batch = mem.shape[0]
    blk = pl.BlockSpec((1, N_MEM, D_MODEL), lambda b: (b, 0, 0))
    sds = jax.ShapeDtypeStruct((batch, N_MEM, D_MODEL), BF16)
    return pl.pallas_call(
        _memkv_kernel,
        out_shape=(sds, sds),
        grid=(batch,),
        in_specs=[blk, _const_spec((1, D_MODEL)), _const_spec(w.shape)],
        out_specs=(blk, blk),
        compiler_params=pltpu.CompilerParams(dimension_semantics=("parallel",),
                                             vmem_limit_bytes=VMEM_LIMIT_BYTES),
        name="memkv",
    )(mem, g, w)


def _post_kernel(h_ref, oft_ref, og_ref, wof_ref, wog_ref, g1_ref, g2_ref, wq_ref, km_ref, vm_ref,
                 wo_ref, g3_ref, out_ref):
    m = _dot_tn(oft_ref[0], wof_ref[...]) + _dot(og_ref[...], wog_ref[...])
    h2 = h_ref[...] + _rms_scale(m) * g1_ref[...]
    u = (_rms_scale(h2) * g2_ref[...]).astype(BF16)
    q = _dot(u, wq_ref[...]) * (MEM_HEAD_DIM ** -0.5)
    heads = []
    for hd in range(MEM_HEADS):
        cols = slice(hd * MEM_HEAD_DIM, (hd + 1) * MEM_HEAD_DIM)
        s = _dot_nt(q[:, cols].astype(BF16), km_ref[0, :, cols])
        p = jnp.exp(s - jnp.max(s, axis=-1, keepdims=True))
        p = p / jnp.sum(p, axis=-1, keepdims=True)
        heads.append(_dot(p.astype(BF16), vm_ref[0, :, cols]))
    o = jnp.concatenate(heads, axis=1).astype(BF16)
    c = _dot(o, wo_ref[...])
    out_ref[...] = h2 + _rms_scale(c) * g3_ref[...]


def _post(h, oft, og, wof, wog, g1, g2, wq, km, vm, wo, g3, batch, seq):
    tm = min(TOKEN_TILE, seq)
    nt = seq // tm
    row = lambda w: pl.BlockSpec((tm, w), lambda b, i: (b * nt + i, 0))
    memblk = pl.BlockSpec((1, N_MEM, D_MODEL), lambda b, i: (b, 0, 0))
    g = _const_spec((1, D_MODEL))
    return pl.pallas_call(
        _post_kernel,
        out_shape=jax.ShapeDtypeStruct((batch * seq, D_MODEL), F32),
        grid=(batch, nt),
        in_specs=[row(D_MODEL), pl.BlockSpec((1, FOX_W, tm), lambda b, i: (b, 0, i)), row(GLA_VW),
                  _const_spec(wof.shape), _const_spec(wog.shape), g, g, _const_spec(wq.shape),
                  memblk, memblk, _const_spec(wo.shape), g],
        out_specs=row(D_MODEL),
        compiler_params=pltpu.CompilerParams(dimension_semantics=("parallel", "parallel"),
                                             vmem_limit_bytes=VMEM_LIMIT_BYTES),
        name="post",
    )(h, oft, og, wof, wog, g1, g2, wq, km, vm, wo, g3)


def kernel(x, mem, ffn1_g_pre, ffn1_w_gu, ffn1_w_down, ffn1_g_post, mix_g_pre, mix_w_in, mix_w_a2, mix_b_a,
           mix_b_f, mix_g_gla, mix_w_out, mix_g_post, mem_g_pre, mem_g_kv, mem_w_q, mem_w_kv, mem_w_o,
           mem_g_post, ffn2_g_pre, ffn2_w_gu, ffn2_w_down, ffn2_g_post):
    batch, seq, _ = x.shape
    depth = ffn1_g_pre.shape[0]
    row = lambda v: v.reshape(1, -1)
    h = x.reshape(batch * seq, D_MODEL)
    for l in range(depth):
        wgu1, wd1 = _prep_ffn(ffn1_w_gu[l], ffn1_w_down[l])
        wgu2, wd2 = _prep_ffn(ffn2_w_gu[l], ffn2_w_down[l])
        w1, wvt, wa2, bfp, ba = _prep_mix(mix_w_in[l], mix_w_a2[l], mix_b_a[l], mix_b_f[l])
        w_out = mix_w_out[l].astype(BF16)

        h = _ffn(h, row(ffn1_g_pre[l]), wgu1, wd1, row(ffn1_g_post[l]))
        qaug, kaug, vt, gq, gk, gv, gr, la = _mix_in(h, row(mix_g_pre[l]), w1, wvt, wa2, bfp, ba, batch, seq)
        oft = _fox(qaug, kaug, vt, batch, seq)
        og = _gla(gq, gk, gv, gr, la, row(mix_g_gla[l]), batch, seq)
        km, vm = _memkv(mem, row(mem_g_kv[l]), mem_w_kv[l].astype(BF16))
        h = _post(h, oft, og, w_out[:FOX_W], w_out[FOX_W:], row(mix_g_post[l]), row(mem_g_pre[l]),
                  mem_w_q[l].astype(BF16), km, vm, mem_w_o[l].astype(BF16), row(mem_g_post[l]), batch, seq)
        h = _ffn(h, row(ffn2_g_pre[l]), wgu2, wd2, row(ffn2_g_post[l]))
    return h.reshape(batch, seq, D_MODEL)
```

```python
import functools

import numpy as np
import jax
import jax.numpy as jnp
from jax import lax
from jax.experimental import pallas as pl
from jax.experimental.pallas import tpu as pltpu

F32 = jnp.float32
BF16 = jnp.bfloat16

D_MODEL = 1024
FOX_HEADS = 8
FOX_HEAD_DIM = 64
GLA_HEADS = 4
GLA_HEAD_K = 64
GLA_HEAD_V = 128
GLA_GATE_RANK = 16
GLA_GATE_TAU = 16.0
GLA_CHUNK = 64
FOX_W = FOX_HEADS * FOX_HEAD_DIM
GLA_KW = GLA_HEADS * GLA_HEAD_K
GLA_VW = GLA_HEADS * GLA_HEAD_V
IN_SPLITS = (FOX_W, FOX_W, FOX_W, FOX_HEADS, GLA_KW, GLA_KW, GLA_VW, GLA_VW, GLA_GATE_RANK)
N_MEM = 256
MEM_HEADS = 4
MEM_HEAD_DIM = D_MODEL // MEM_HEADS
D_FF = 2816
MACARON_W = 0.5
RMS_EPS = 1e-6
LOG2E = 1.4426950408889634

LANES = 128
VMEM_LIMIT_BYTES = 56 * 1024 * 1024

TOKEN_TILE = 512
FFN_CHUNK = 256
FOX_TQ = 512
FOX_TK = 512
FOX_HEADS_PER_STEP = 4
FOX_LAZY_HEADROOM = 60.0
FOX_VROWS = FOX_HEAD_DIM + 16
GLA_TILE = 512

AUG_HI, AUG_MID, AUG_LO, AUG_ONE = 0, 8, 16, 24
GA_LANE = 8


def _rms_scale(x):
    return x * lax.rsqrt(jnp.mean(x * x, axis=-1, keepdims=True) + RMS_EPS)


def _sigmoid(x):
    return 1.0 / (1.0 + jnp.exp(-x))


def _log_sigmoid(x):
    return jnp.minimum(x, 0.0) - jnp.log(1.0 + jnp.exp(-jnp.abs(x)))


def _dot(a, b):
    return jnp.dot(a, b, preferred_element_type=F32)


def _dot_nt(a, b):
    return lax.dot_general(a, b, (((1,), (1,)), ((), ())), preferred_element_type=F32)


def _dot_tn(a, b):
    return lax.dot_general(a, b, (((0,), (0,)), ((), ())), preferred_element_type=F32)


def _split3(x):
    hi = x.astype(BF16)
    r1 = x - hi.astype(F32)
    mid = r1.astype(BF16)
    lo = (r1 - mid.astype(F32)).astype(BF16)
    return hi, mid, lo


def _const_spec(shape):
    zeros = (0,) * len(shape)
    return pl.BlockSpec(shape, lambda *_: zeros, pipeline_mode=pl.Buffered(1))


def _ffn_math(h, g_pre, wgu_ref, wd_ref, g_post):
    xn = (_rms_scale(h) * g_pre).astype(BF16)
    n_chunks = D_FF // FFN_CHUNK
    acc = None
    for j in range(n_chunks):
        gate = _dot(xn, wgu_ref[:, j * FFN_CHUNK:(j + 1) * FFN_CHUNK])
        up = _dot(xn, wgu_ref[:, D_FF + j * FFN_CHUNK:D_FF + (j + 1) * FFN_CHUNK])
        act = (gate * _sigmoid(gate) * up).astype(BF16)
        part = _dot(act, wd_ref[j * FFN_CHUNK:(j + 1) * FFN_CHUNK, :])
        acc = part if acc is None else acc + part
    return h + MACARON_W * (_rms_scale(acc) * g_post)


def _ffn_kernel(h_ref, gpre_ref, wgu_ref, wd_ref, gpost_ref, o_ref):
    o_ref[...] = _ffn_math(h_ref[...], gpre_ref[...], wgu_ref, wd_ref, gpost_ref[...])


def _prep_ffn(w_gu, w_down):
    return w_gu.astype(BF16), w_down.astype(BF16)


def _ffn(h, g_pre, wgu, wd, g_post):
    t = h.shape[0]
    tm = min(TOKEN_TILE, t)
    row = pl.BlockSpec((tm, D_MODEL), lambda i: (i, 0))
    return pl.pallas_call(
        _ffn_kernel,
        out_shape=jax.ShapeDtypeStruct((t, D_MODEL), F32),
        grid=(t // tm,),
        in_specs=[row, _const_spec((1, D_MODEL)), _const_spec(wgu.shape), _const_spec(wd.shape),
                  _const_spec((1, D_MODEL))],
        out_specs=row,
        compiler_params=pltpu.CompilerParams(dimension_semantics=("parallel",),
                                             vmem_limit_bytes=VMEM_LIMIT_BYTES),
        name="ffn",
    )(h, g_pre, wgu, wd, g_post)


MIX_QW = FOX_HEADS * LANES
MIX_G0 = 2 * FOX_W
MIX_GW = 2 * GLA_KW + 2 * GLA_VW + LANES


def _prep_mix(w_in, w_a2, b_a, b_f):
    offs = np.cumsum(IN_SPLITS)[:-1].tolist()
    fq, fk, fv, fz, gq, gk, gv, gr, ga = jnp.split(w_in, offs, axis=1)
    pad = jnp.zeros((D_MODEL, LANES - FOX_HEADS - GLA_GATE_RANK), F32)
    w1 = jnp.concatenate([fq, fk, gq, gk, gv, gr, fz, ga, pad], axis=1).astype(BF16)
    wvt = fv.T.astype(BF16)
    wa2 = jnp.zeros((LANES, GLA_KW), F32).at[GA_LANE:GA_LANE + GLA_GATE_RANK].set(w_a2).astype(BF16)
    bfp = jnp.zeros((1, LANES), F32).at[0, :FOX_HEADS].set(b_f)
    return w1, wvt, wa2, bfp, b_a.reshape(1, GLA_KW)


def _aug_base(h):
    return FOX_HEAD_DIM if h % 2 == 0 else 0


def _aug_placement():
    p = np.zeros((LANES, 2 * MIX_QW), np.float32)
    for h in range(FOX_HEADS):
        qb = h * LANES + _aug_base(h)
        kb = MIX_QW + h * LANES + _aug_base(h)
        for i, src in enumerate((AUG_HI, AUG_MID, AUG_LO)):
            p[src + h, qb + i] = 1.0
            p[AUG_ONE, kb + i] = 1.0
            p[AUG_ONE, qb + 3 + i] = 1.0
            p[src + h, kb + 3 + i] = -1.0
    return jnp.asarray(p, BF16)


def _mix_in_kernel(h_ref, g_ref, w1_ref, wvt_ref, wa2_ref, bf_ref, ba_ref, p_ref, tri_ref,
                   qaug_ref, kaug_ref, vt_ref, gq_ref, gk_ref, gv_ref, gr_ref, la_ref, carry_ref):
    @pl.when(pl.program_id(1) == 0)
    def _():
        carry_ref[...] = jnp.zeros_like(carry_ref)

    tm = h_ref.shape[0]
    xn = (_rms_scale(h_ref[...]) * g_ref[...]).astype(BF16)
    zq = _dot(xn, w1_ref[:, :FOX_W]) * (FOX_HEAD_DIM ** -0.5 * LOG2E)
    zk = _dot(xn, w1_ref[:, FOX_W:MIX_G0])
    zg = _dot(xn, w1_ref[:, MIX_G0:])
    vt = _dot_nt(wvt_ref[...], xn).astype(BF16)
    for hd in range(FOX_HEADS):
        vt_ref[0, hd * FOX_VROWS:hd * FOX_VROWS + FOX_HEAD_DIM, :] = vt[hd * FOX_HEAD_DIM:(hd + 1) * FOX_HEAD_DIM]
        vt_ref[0, hd * FOX_VROWS + FOX_HEAD_DIM:(hd + 1) * FOX_VROWS, :] = jnp.ones(
            (FOX_VROWS - FOX_HEAD_DIM, tm), BF16)

    o = 0
    gq_ref[...] = zg[:, o:o + GLA_KW]; o += GLA_KW
    gk_ref[...] = zg[:, o:o + GLA_KW]; o += GLA_KW
    gv_ref[...] = zg[:, o:o + GLA_VW].astype(BF16); o += GLA_VW
    gr_ref[...] = zg[:, o:o + GLA_VW]; o += GLA_VW
    small = zg[:, o:o + LANES]

    xa = _dot(small.astype(BF16), wa2_ref[...]) + ba_ref[...]
    la_ref[...] = _log_sigmoid(xa) * (1.0 / GLA_GATE_TAU)

    lane = lax.broadcasted_iota(jnp.int32, (tm, LANES), 1)
    logf = jnp.where(lane < FOX_HEADS, _log_sigmoid(small + bf_ref[...]), 0.0)
    tri = tri_ref[...]
    l_hi, l_mid, l_lo = _split3(logf)
    cum = _dot(tri, l_hi) + _dot(tri, l_mid) + _dot(tri, l_lo) + carry_ref[...]
    carry_ref[...] = cum[tm - 1:tm, :]

    c_hi, c_mid, c_lo = _split3(cum * LOG2E)
    packed = (c_hi.astype(F32) + pltpu.roll(c_mid.astype(F32), AUG_MID, 1)
              + pltpu.roll(c_lo.astype(F32), AUG_LO, 1))
    packed = jnp.where(lane == AUG_ONE, 1.0, packed)
    aug = _dot(packed.astype(BF16), p_ref[...])
    low_half = lane < FOX_HEAD_DIM
    for hd in range(FOX_HEADS):
        pair = slice((hd // 2) * LANES, (hd // 2 + 1) * LANES)
        cols = slice(hd * LANES, (hd + 1) * LANES)
        own = low_half if hd % 2 == 0 else jnp.logical_not(low_half)
        qaug_ref[:, cols] = (jnp.where(own, zq[:, pair], 0.0) + aug[:, cols]).astype(BF16)
        kaug_ref[:, cols] = (jnp.where(own, zk[:, pair], 0.0)
                             + aug[:, MIX_QW + hd * LANES:MIX_QW + (hd + 1) * LANES]).astype(BF16)


def _mix_in(h, g, w1, wvt, wa2, bfp, ba, batch, seq):
    tm = min(TOKEN_TILE, seq)
    nt = seq // tm
    tri = jnp.asarray(np.tril(np.ones((tm, tm), np.float32)), BF16)
    row = lambda w: pl.BlockSpec((tm, w), lambda b, i: (b * nt + i, 0))
    t = batch * seq
    sds = jax.ShapeDtypeStruct
    return pl.pallas_call(
        _mix_in_kernel,
        out_shape=(sds((t, MIX_QW), BF16), sds((t, MIX_QW), BF16),
                   sds((batch, FOX_HEADS * FOX_VROWS, seq), BF16),
                   sds((t, GLA_KW), F32), sds((t, GLA_KW), F32), sds((t, GLA_VW), BF16),
                   sds((t, GLA_VW), F32), sds((t, GLA_KW), F32)),
        grid=(batch, nt),
        in_specs=[row(D_MODEL), _const_spec((1, D_MODEL)), _const_spec(w1.shape), _const_spec(wvt.shape),
                  _const_spec(wa2.shape), _const_spec(bfp.shape), _const_spec(ba.shape),
                  _const_spec((LANES, 2 * MIX_QW)), _const_spec((tm, tm))],
        out_specs=(row(MIX_QW), row(MIX_QW),
                   pl.BlockSpec((1, FOX_HEADS * FOX_VROWS, tm), lambda b, i: (b, 0, i)),
                   row(GLA_KW), row(GLA_KW), row(GLA_VW), row(GLA_VW), row(GLA_KW)),
        scratch_shapes=[pltpu.VMEM((1, LANES), F32)],
        compiler_params=pltpu.CompilerParams(dimension_semantics=("arbitrary", "arbitrary"),
                                             vmem_limit_bytes=VMEM_LIMIT_BYTES),
        name="mix_in",
    )(h, g, w1, wvt, wa2, bfp, ba, _aug_placement(), tri)


def _fox_kernel(q_ref, k_ref, vt_ref, o_ref, p_ref, *, tq, tk, heads):
    qi = pl.program_id(2)
    dh = FOX_HEAD_DIM
    vr = FOX_VROWS
    assert tq == tk

    def causal(s, ks):
        kpos = ks + lax.broadcasted_iota(jnp.int32, (tk, tq), 0)
        qpos = qi * tq + lax.broadcasted_iota(jnp.int32, (tk, tq), 1)
        return jnp.where(kpos <= qpos, s, -jnp.inf)

    def scores(hh, ks):
        cols = slice(hh * LANES, (hh + 1) * LANES)
        return _dot_nt(k_ref[pl.ds(ks, tk), cols], q_ref[:, cols])

    def exact_step(hh, j, carry, masked):
        m, l, acc = carry
        ks = pl.multiple_of(j * tk, tk)
        s = scores(hh, ks)
        if masked:
            s = causal(s, ks)
        m_new = jnp.maximum(m, jnp.max(s, axis=0, keepdims=True))
        p = jnp.exp2(s - m_new)
        alpha = jnp.exp2(m - m_new)
        l = alpha * l + jnp.sum(p, axis=0, keepdims=True)
        vt = vt_ref[0, hh * vr:hh * vr + dh, pl.ds(ks, tk)]
        acc = alpha * acc + _dot(vt, p.astype(BF16))
        return m_new, l, acc

    def weighted_values(hh, ks):
        return _dot(vt_ref[0, hh * vr:(hh + 1) * vr, pl.ds(ks, tk)], p_ref[hh])

    def lazy_block(ks, ks_prev, carry, masked):
        out = []
        for hh in range(heads):
            m, acc, over, shift_prev = carry[hh]
            s = scores(hh, ks)
            acc = (acc + weighted_values(hh, ks_prev)) * jnp.exp2(-shift_prev)
            p = jnp.exp2((causal(s, ks) if masked else s) - m).astype(BF16)
            p_ref[hh] = p
            over_b = jnp.log2(jnp.max(p, axis=0, keepdims=True).astype(F32))
            shift = jnp.maximum(over_b, 0.0)
            out.append((m + shift, acc, jnp.maximum(over, over_b), shift))
        return tuple(out)

    def finish(hh, l, acc):
        o_ref[0, hh * dh:(hh + 1) * dh, :] = (acc / l).astype(BF16)

    kd = pl.multiple_of(qi * tk, tk)
    ones = jnp.ones((16, LANES), BF16)
    carry = []
    for hh in range(heads):
        cols = slice(hh * LANES, (hh + 1) * LANES)
        own = _dot_nt(ones, q_ref[:, cols] * k_ref[pl.ds(kd, tk), cols])[0:1]
        zero = jnp.zeros((1, tq), F32)
        carry.append((own, jnp.zeros((vr, tq), F32), zero, zero))
    p_ref[...] = jnp.zeros_like(p_ref)

    def body(j, c):
        prev = jnp.maximum(j - 1, 0)
        return lazy_block(pl.multiple_of(j * tk, tk), pl.multiple_of(prev * tk, tk), c, False)

    carry = lax.fori_loop(0, qi, body, tuple(carry))
    carry = lazy_block(kd, pl.multiple_of(jnp.maximum(qi - 1, 0) * tk, tk), carry, True)
    worst = jnp.max(jnp.concatenate([c[2] for c in carry], axis=0))
    for hh in range(heads):
        acc = carry[hh][1] + weighted_values(hh, kd)
        finish(hh, acc[dh:dh + 1], acc[:dh])

    @pl.when(jnp.logical_not(worst <= FOX_LAZY_HEADROOM))
    def _():
        init = (jnp.full((1, tq), -jnp.inf, F32), jnp.zeros((1, tq), F32), jnp.zeros((dh, tq), F32))
        for hh in range(heads):
            c = exact_step(hh, qi, init, True)
            c = lax.fori_loop(0, qi, lambda j, c, hh=hh: exact_step(hh, j, c, False), c)
            finish(hh, c[1], c[2])


def _fox(qaug, kaug, vt, batch, seq):
    tq = min(FOX_TQ, seq)
    tk = min(FOX_TK, tq)
    nq = seq // tq
    hp = FOX_HEADS_PER_STEP
    return pl.pallas_call(
        functools.partial(_fox_kernel, tq=tq, tk=tk, heads=hp),
        out_shape=jax.ShapeDtypeStruct((batch, FOX_W, seq), BF16),
        grid=(batch, FOX_HEADS // hp, nq),
        in_specs=[pl.BlockSpec((tq, hp * LANES), lambda b, h, i: (b * nq + i, h)),
                  pl.BlockSpec((seq, hp * LANES), lambda b, h, i: (b, h), pipeline_mode=pl.Buffered(1)),
                  pl.BlockSpec((1, hp * FOX_VROWS, seq), lambda b, h, i: (b, h, 0),
                               pipeline_mode=pl.Buffered(1))],
        out_specs=pl.BlockSpec((1, hp * FOX_HEAD_DIM, tq), lambda b, h, i: (b, h, i)),
        scratch_shapes=[pltpu.VMEM((hp, tk, tq), BF16)],
        compiler_params=pltpu.CompilerParams(dimension_semantics=("parallel", "parallel", "arbitrary"),
                                             vmem_limit_bytes=VMEM_LIMIT_BYTES),
        name="fox",
    )(qaug, kaug, vt)


def _gla_kernel(gq_ref, gk_ref, gv_ref, gr_ref, la_ref, g_ref, tri_ref, o_ref, st_ref, *, nb, tile):
    @pl.when(pl.program_id(0) == 0)
    def _():
        st_ref[...] = jnp.zeros_like(st_ref)

    c = GLA_CHUNK
    lane_head = lax.broadcasted_iota(jnp.int32, (c, GLA_KW), 1) // GLA_HEAD_K
    st_head = lax.broadcasted_iota(jnp.int32, (GLA_HEAD_V, GLA_KW), 1) // GLA_HEAD_K
    causal = (lax.broadcasted_iota(jnp.int32, (c, c), 0) >= lax.broadcasted_iota(jnp.int32, (c, c), 1))
    tri = tri_ref[...]

    def chunk(ci, _):
        rows = pl.ds(pl.multiple_of(ci * c, c), c)
        bh = [(b, h) for b in range(nb) for h in range(GLA_HEADS)]
        vcols = lambda h: slice(h * GLA_HEAD_V, (h + 1) * GLA_HEAD_V)
        bcs = []
        for b in range(nb):
            la = la_ref[b, rows, :]
            la_hi = la.astype(BF16)
            la_lo = (la - la_hi.astype(F32)).astype(BF16)
            bcs.append(_dot(tri, la_hi) + _dot(tri, la_lo))
        qd, kd, ke, st, st_lo, decay = [], [], [], [], [], []
        for b in range(nb):
            bc = bcs[b]
            bl = bc[c - 1:c, :]
            k = gk_ref[b, rows, :]
            qd.append(gq_ref[b, rows, :] * (GLA_HEAD_K ** -0.5) * jnp.exp(bc))
            kd.append((k * jnp.exp(-bc)).astype(BF16))
            ke.append((k * jnp.exp(bl - bc)).astype(BF16))
            decay.append(jnp.exp(bl))
            st.append(st_ref[b])
            st_lo.append(st[b].astype(BF16))
        qdh = {(b, h): jnp.where(lane_head == h, qd[b], 0.0).astype(BF16) for b, h in bh}
        a = {(b, h): jnp.where(causal, _dot_nt(qdh[b, h], kd[b]), 0.0).astype(BF16) for b, h in bh}
        vh = {(b, h): gv_ref[b, rows, vcols(h)] for b, h in bh}
        o = {(b, h): _dot(a[b, h], vh[b, h]) + _dot_nt(qdh[b, h], st_lo[b]) for b, h in bh}
        u = {(b, h): _dot_tn(vh[b, h], ke[b]) for b, h in bh}
        for b, h in bh:
            gr = gr_ref[b, rows, vcols(h)]
            o_ref[b, rows, vcols(h)] = (_rms_scale(o[b, h]) * g_ref[:, vcols(h)]
                                        * (gr * _sigmoid(gr))).astype(BF16)
        for b in range(nb):
            upd = sum(jnp.where(st_head == h, u[b, h], 0.0) for h in range(GLA_HEADS))
            st_ref[b] = st[b] * decay[b] + upd
        return 0

    lax.fori_loop(0, tile // c, chunk, 0)


def _gla(gq, gk, gv, gr, la, g_gla, batch, seq):
    tile = min(GLA_TILE, seq)
    tri = jnp.asarray(np.tril(np.ones((GLA_CHUNK, GLA_CHUNK), np.float32)), BF16)
    blk = lambda w: pl.BlockSpec((batch, tile, w), lambda i: (0, i, 0))
    r3 = lambda x: x.reshape(batch, seq, x.shape[-1])
    out = pl.pallas_call(
        functools.partial(_gla_kernel, nb=batch, tile=tile),
        out_shape=jax.ShapeDtypeStruct((batch, seq, GLA_VW), BF16),
        grid=(seq // tile,),
        in_specs=[blk(GLA_KW), blk(GLA_KW), blk(GLA_VW), blk(GLA_VW), blk(GLA_KW),
                  _const_spec((1, GLA_VW)), _const_spec((GLA_CHUNK, GLA_CHUNK))],
        out_specs=blk(GLA_VW),
        scratch_shapes=[pltpu.VMEM((batch, GLA_HEAD_V, GLA_KW), F32)],
        compiler_params=pltpu.CompilerParams(dimension_semantics=("arbitrary",),
                                             vmem_limit_bytes=VMEM_LIMIT_BYTES),
        name="gla",
    )(r3(gq), r3(gk), r3(gv), r3(gr), r3(la), g_gla, tri)
    return out.reshape(batch * seq, GLA_VW)


def _memkv_kernel(mem_ref, g_ref, w_ref, k_ref, v_ref):
    mn = (_rms_scale(mem_ref[0]) * g_ref[...]).astype(BF16)
    kv = _dot(mn, w_ref[...])
    k_ref[0] = kv[:, :D_MODEL].astype(BF16)
    v_ref[0] = kv[:, D_MODEL:].astype(BF16)


def _memkv(mem, g, w):
    batch = mem.shape[0]
    blk = pl.BlockSpec((1, N_MEM, D_MODEL), lambda b: (b, 0, 0))
    sds = jax.ShapeDtypeStruct((batch, N_MEM, D_MODEL), BF16)
    return pl.pallas_call(
        _memkv_kernel,
        out_shape=(sds, sds),
        grid=(batch,),
        in_specs=[blk, _const_spec((1, D_MODEL)), _const_spec(w.shape)],
        out_specs=(blk, blk),
        compiler_params=pltpu.CompilerParams(dimension_semantics=("parallel",),
                                             vmem_limit_bytes=VMEM_LIMIT_BYTES),
        name="memkv",
    )(mem, g, w)


def _post_kernel(h_ref, oft_ref, og_ref, wof_ref, wog_ref, g1_ref, g2_ref, wq_ref, km_ref, vm_ref,
                 wo_ref, g3_ref, out_ref):
    m = _dot_tn(oft_ref[0], wof_ref[...]) + _dot(og_ref[...], wog_ref[...])
    h2 = h_ref[...] + _rms_scale(m) * g1_ref[...]
    u = (_rms_scale(h2) * g2_ref[...]).astype(BF16)
    q = _dot(u, wq_ref[...]) * (MEM_HEAD_DIM ** -0.5)
    heads = []
    for hd in range(MEM_HEADS):
        cols = slice(hd * MEM_HEAD_DIM, (hd + 1) * MEM_HEAD_DIM)
        s = _dot_nt(q[:, cols].astype(BF16), km_ref[0, :, cols])
        p = jnp.exp(s - jnp.max(s, axis=-1, keepdims=True))
        p = p / jnp.sum(p, axis=-1, keepdims=True)
        heads.append(_dot(p.astype(BF16), vm_ref[0, :, cols]))
    o = jnp.concatenate(heads, axis=1).astype(BF16)
    c = _dot(o, wo_ref[...])
    out_ref[...] = h2 + _rms_scale(c) * g3_ref[...]


def _post(h, oft, og, wof, wog, g1, g2, wq, km, vm, wo, g3, batch, seq):
    tm = min(TOKEN_TILE, seq)
    nt = seq // tm
    row = lambda w: pl.BlockSpec((tm, w), lambda b, i: (b * nt + i, 0))
    memblk = pl.BlockSpec((1, N_MEM, D_MODEL), lambda b, i: (b, 0, 0))
    g = _const_spec((1, D_MODEL))
    return pl.pallas_call(
        _post_kernel,
        out_shape=jax.ShapeDtypeStruct((batch * seq, D_MODEL), F32),
        grid=(batch, nt),
        in_specs=[row(D_MODEL), pl.BlockSpec((1, FOX_W, tm), lambda b, i: (b, 0, i)), row(GLA_VW),
                  _const_spec(wof.shape), _const_spec(wog.shape), g, g, _const_spec(wq.shape),
                  memblk, memblk, _const_spec(wo.shape), g],
        out_specs=row(D_MODEL),
        compiler_params=pltpu.CompilerParams(dimension_semantics=("parallel", "parallel"),
                                             vmem_limit_bytes=VMEM_LIMIT_BYTES),
        name="post",
    )(h, oft, og, wof, wog, g1, g2, wq, km, vm, wo, g3)


def kernel(x, mem, ffn1_g_pre, ffn1_w_gu, ffn1_w_down, ffn1_g_post, mix_g_pre, mix_w_in, mix_w_a2, mix_b_a,
           mix_b_f, mix_g_gla, mix_w_out, mix_g_post, mem_g_pre, mem_g_kv, mem_w_q, mem_w_kv, mem_w_o,
           mem_g_post, ffn2_g_pre, ffn2_w_gu, ffn2_w_down, ffn2_g_post):
    batch, seq, _ = x.shape
    depth = ffn1_g_pre.shape[0]
    row = lambda v: v.reshape(1, -1)
    h = x.reshape(batch * seq, D_MODEL)
    for l in range(depth):
        wgu1, wd1 = _prep_ffn(ffn1_w_gu[l], ffn1_w_down[l])
        wgu2, wd2 = _prep_ffn(ffn2_w_gu[l], ffn2_w_down[l])
        w1, wvt, wa2, bfp, ba = _prep_mix(mix_w_in[l], mix_w_a2[l], mix_b_a[l], mix_b_f[l])
        w_out = mix_w_out[l].astype(BF16)

        h = _ffn(h, row(ffn1_g_pre[l]), wgu1, wd1, row(ffn1_g_post[l]))
        qaug, kaug, vt, gq, gk, gv, gr, la = _mix_in(h, row(mix_g_pre[l]), w1, wvt, wa2, bfp, ba, batch, seq)
        oft = _fox(qaug, kaug, vt, batch, seq)
        og = _gla(gq, gk, gv, gr, la, row(mix_g_gla[l]), batch, seq)
        km, vm = _memkv(mem, row(mem_g_kv[l]), mem_w_kv[l].astype(BF16))
        h = _post(h, oft, og, w_out[:FOX_W], w_out[FOX_W:], row(mix_g_post[l]), row(mem_g_pre[l]),
                  mem_w_q[l].astype(BF16), km, vm, mem_w_o[l].astype(BF16), row(mem_g_post[l]), batch, seq)
        h = _ffn(h, row(ffn2_g_pre[l]), wgu2, wd2, row(ffn2_g_post[l]))
    return h.reshape(batch, seq, D_MODEL)
```

```python
import functools

import numpy as np
import jax
import jax.numpy as jnp
from jax import lax
from jax.experimental import pallas as pl
from jax.experimental.pallas import tpu as pltpu

F32 = jnp.float32
BF16 = jnp.bfloat16

D_MODEL = 1024
FOX_HEADS = 8
FOX_HEAD_DIM = 64
GLA_HEADS = 4
GLA_HEAD_K = 64
GLA_HEAD_V = 128
GLA_GATE_RANK = 16
GLA_GATE_TAU = 16.0
GLA_CHUNK = 64
FOX_W = FOX_HEADS * FOX_HEAD_DIM
GLA_KW = GLA_HEADS * GLA_HEAD_K
GLA_VW = GLA_HEADS * GLA_HEAD_V
IN_SPLITS = (FOX_W, FOX_W, FOX_W, FOX_HEADS, GLA_KW, GLA_KW, GLA_VW, GLA_VW, GLA_GATE_RANK)
N_MEM = 256
MEM_HEADS = 4
MEM_HEAD_DIM = D_MODEL // MEM_HEADS
D_FF = 2816
MACARON_W = 0.5
RMS_EPS = 1e-6
LOG2E = 1.4426950408889634

LANES = 128
VMEM_LIMIT_BYTES = 56 * 1024 * 1024

TOKEN_TILE = 512
FFN_CHUNK = 256
FOX_TQ = 512
FOX_TK = 512
FOX_HEADS_PER_STEP = 4
FOX_LAZY_HEADROOM = 60.0
FOX_VROWS = FOX_HEAD_DIM + 16
GLA_TILE = 512

AUG_HI, AUG_MID, AUG_LO, AUG_ONE = 0, 8, 16, 24
GA_LANE = 8


def _rms_scale(x):
    return x * lax.rsqrt(jnp.mean(x * x, axis=-1, keepdims=True) + RMS_EPS)


def _sigmoid(x):
    return 1.0 / (1.0 + jnp.exp(-x))


def _log_sigmoid(x):
    return jnp.minimum(x, 0.0) - jnp.log(1.0 + jnp.exp(-jnp.abs(x)))


def _dot(a, b):
    return jnp.dot(a, b, preferred_element_type=F32)


def _dot_nt(a, b):
    return lax.dot_general(a, b, (((1,), (1,)), ((), ())), preferred_element_type=F32)


def _dot_tn(a, b):
    return lax.dot_general(a, b, (((0,), (0,)), ((), ())), preferred_element_type=F32)


def _split3(x):
    hi = x.astype(BF16)
    r1 = x - hi.astype(F32)
    mid = r1.astype(BF16)
    lo = (r1 - mid.astype(F32)).astype(BF16)
    return hi, mid, lo


def _const_spec(shape):
    zeros = (0,) * len(shape)
    return pl.BlockSpec(shape, lambda *_: zeros, pipeline_mode=pl.Buffered(1))


def _ffn_math(h, g_pre, wgu_ref, wd_ref, g_post):
    xn = (_rms_scale(h) * g_pre).astype(BF16)
    n_chunks = D_FF // FFN_CHUNK
    acc = None
    for j in range(n_chunks):
        gate = _dot(xn, wgu_ref[:, j * FFN_CHUNK:(j + 1) * FFN_CHUNK])
        up = _dot(xn, wgu_ref[:, D_FF + j * FFN_CHUNK:D_FF + (j + 1) * FFN_CHUNK])
        act = (gate * _sigmoid(gate) * up).astype(BF16)
        part = _dot(act, wd_ref[j * FFN_CHUNK:(j + 1) * FFN_CHUNK, :])
        acc = part if acc is None else acc + part
    return h + MACARON_W * (_rms_scale(acc) * g_post)


def _ffn_kernel(h_ref, gpre_ref, wgu_ref, wd_ref, gpost_ref, o_ref):
    o_ref[...] = _ffn_math(h_ref[...], gpre_ref[...], wgu_ref, wd_ref, gpost_ref[...])


def _prep_ffn(w_gu, w_down):
    return w_gu.astype(BF16), w_down.astype(BF16)


def _ffn(h, g_pre, wgu, wd, g_post):
    t = h.shape[0]
    tm = min(TOKEN_TILE, t)
    row = pl.BlockSpec((tm, D_MODEL), lambda i: (i, 0))
    return pl.pallas_call(
        _ffn_kernel,
        out_shape=jax.ShapeDtypeStruct((t, D_MODEL), F32),
        grid=(t // tm,),
        in_specs=[row, _const_spec((1, D_MODEL)), _const_spec(wgu.shape), _const_spec(wd.shape),
                  _const_spec((1, D_MODEL))],
        out_specs=row,
        compiler_params=pltpu.CompilerParams(dimension_semantics=("parallel",),
                                             vmem_limit_bytes=VMEM_LIMIT_BYTES),
        name="ffn",
    )(h, g_pre, wgu, wd, g_post)


MIX_QW = FOX_HEADS * LANES
MIX_G0 = 2 * FOX_W
MIX_GW = 2 * GLA_KW + 2 * GLA_VW + LANES


def _prep_mix(w_in, w_a2, b_a, b_f):
    offs = np.cumsum(IN_SPLITS)[:-1].tolist()
    fq, fk, fv, fz, gq, gk, gv, gr, ga = jnp.split(w_in, offs, axis=1)
    pad = jnp.zeros((D_MODEL, LANES - FOX_HEADS - GLA_GATE_RANK), F32)
    w1 = jnp.concatenate([fq, fk, gq, gk, gv, gr, fz, ga, pad], axis=1).astype(BF16)
    wvt = fv.T.astype(BF16)
    wa2 = jnp.zeros((LANES, GLA_KW), F32).at[GA_LANE:GA_LANE + GLA_GATE_RANK].set(w_a2).astype(BF16)
    bfp = jnp.zeros((1, LANES), F32).at[0, :FOX_HEADS].set(b_f)
    return w1, wvt, wa2, bfp, b_a.reshape(1, GLA_KW)


def _aug_base(h):
    return FOX_HEAD_DIM if h % 2 == 0 else 0


def _aug_placement():
    p = np.zeros((LANES, 2 * MIX_QW), np.float32)
    for h in range(FOX_HEADS):
        qb = h * LANES + _aug_base(h)
        kb = MIX_QW + h * LANES + _aug_base(h)
        for i, src in enumerate((AUG_HI, AUG_MID, AUG_LO)):
            p[src + h, qb + i] = 1.0
            p[AUG_ONE, kb + i] = 1.0
            p[AUG_ONE, qb + 3 + i] = 1.0
            p[src + h, kb + 3 + i] = -1.0
    return jnp.asarray(p, BF16)


def _mix_in_kernel(h_ref, g_ref, w1_ref, wvt_ref, wa2_ref, bf_ref, ba_ref, p_ref, tri_ref,
                   qaug_ref, kaug_ref, vt_ref, gq_ref, gk_ref, gv_ref, gr_ref, la_ref, carry_ref):
    @pl.when(pl.program_id(1) == 0)
    def _():
        carry_ref[...] = jnp.zeros_like(carry_ref)

    tm = h_ref.shape[0]
    xn = (_rms_scale(h_ref[...]) * g_ref[...]).astype(BF16)
    zq = _dot(xn, w1_ref[:, :FOX_W]) * (FOX_HEAD_DIM ** -0.5 * LOG2E)
    zk = _dot(xn, w1_ref[:, FOX_W:MIX_G0])
    zg = _dot(xn, w1_ref[:, MIX_G0:])
    vt = _dot_nt(wvt_ref[...], xn).astype(BF16)
    for hd in range(FOX_HEADS):
        vt_ref[0, hd * FOX_VROWS:hd * FOX_VROWS + FOX_HEAD_DIM, :] = vt[hd * FOX_HEAD_DIM:(hd + 1) * FOX_HEAD_DIM]
        vt_ref[0, hd * FOX_VROWS + FOX_HEAD_DIM:(hd + 1) * FOX_VROWS, :] = jnp.ones(
            (FOX_VROWS - FOX_HEAD_DIM, tm), BF16)

    o = 0
    gq_ref[...] = zg[:, o:o + GLA_KW]; o += GLA_KW
    gk_ref[...] = zg[:, o:o + GLA_KW]; o += GLA_KW
    gv_ref[...] = zg[:, o:o + GLA_VW].astype(BF16); o += GLA_VW
    gr_ref[...] = zg[:, o:o + GLA_VW]; o += GLA_VW
    small = zg[:, o:o + LANES]

    xa = _dot(small.astype(BF16), wa2_ref[...]) + ba_ref[...]
    la_ref[...] = _log_sigmoid(xa) * (1.0 / GLA_GATE_TAU)

    lane = lax.broadcasted_iota(jnp.int32, (tm, LANES), 1)
    logf = jnp.where(lane < FOX_HEADS, _log_sigmoid(small + bf_ref[...]), 0.0)
    tri = tri_ref[...]
    l_hi, l_mid, l_lo = _split3(logf)
    cum = _dot(tri, l_hi) + _dot(tri, l_mid) + _dot(tri, l_lo) + carry_ref[...]
    carry_ref[...] = cum[tm - 1:tm, :]

    c_hi, c_mid, c_lo = _split3(cum * LOG2E)
    packed = (c_hi.astype(F32) + pltpu.roll(c_mid.astype(F32), AUG_MID, 1)
              + pltpu.roll(c_lo.astype(F32), AUG_LO, 1))
    packed = jnp.where(lane == AUG_ONE, 1.0, packed)
    aug = _dot(packed.astype(BF16), p_ref[...])
    low_half = lane < FOX_HEAD_DIM
    for hd in range(FOX_HEADS):
        pair = slice((hd // 2) * LANES, (hd // 2 + 1) * LANES)
        cols = slice(hd * LANES, (hd + 1) * LANES)
        own = low_half if hd % 2 == 0 else jnp.logical_not(low_half)
        qaug_ref[:, cols] = (jnp.where(own, zq[:, pair], 0.0) + aug[:, cols]).astype(BF16)
        kaug_ref[:, cols] = (jnp.where(own, zk[:, pair], 0.0)
                             + aug[:, MIX_QW + hd * LANES:MIX_QW + (hd + 1) * LANES]).astype(BF16)


def _mix_in(h, g, w1, wvt, wa2, bfp, ba, batch, seq):
    tm = min(TOKEN_TILE, seq)
    nt = seq // tm
    tri = jnp.asarray(np.tril(np.ones((tm, tm), np.float32)), BF16)
    row = lambda w: pl.BlockSpec((tm, w), lambda b, i: (b * nt + i, 0))
    t = batch * seq
    sds = jax.ShapeDtypeStruct
    return pl.pallas_call(
        _mix_in_kernel,
        out_shape=(sds((t, MIX_QW), BF16), sds((t, MIX_QW), BF16),
                   sds((batch, FOX_HEADS * FOX_VROWS, seq), BF16),
                   sds((t, GLA_KW), F32), sds((t, GLA_KW), F32), sds((t, GLA_VW), BF16),
                   sds((t, GLA_VW), F32), sds((t, GLA_KW), F32)),
        grid=(batch, nt),
        in_specs=[row(D_MODEL), _const_spec((1, D_MODEL)), _const_spec(w1.shape), _const_spec(wvt.shape),
                  _const_spec(wa2.shape), _const_spec(bfp.shape), _const_spec(ba.shape),
                  _const_spec((LANES, 2 * MIX_QW)), _const_spec((tm, tm))],
        out_specs=(row(MIX_QW), row(MIX_QW),
                   pl.BlockSpec((1, FOX_HEADS * FOX_VROWS, tm), lambda b, i: (b, 0, i)),
                   row(GLA_KW), row(GLA_KW), row(GLA_VW), row(GLA_VW), row(GLA_KW)),
        scratch_shapes=[pltpu.VMEM((1, LANES), F32)],
        compiler_params=pltpu.CompilerParams(dimension_semantics=("arbitrary", "arbitrary"),
                                             vmem_limit_bytes=VMEM_LIMIT_BYTES),
        name="mix_in",
    )(h, g, w1, wvt, wa2, bfp, ba, _aug_placement(), tri)


def _fox_kernel(q_ref, k_ref, vt_ref, mask_ref, o_ref, p_ref, *, tq, tk, heads):
    qi = pl.program_id(2)
    dh = FOX_HEAD_DIM
    vr = FOX_VROWS
    assert tq == tk

    def scores(hh, ks):
        cols = slice(hh * LANES, (hh + 1) * LANES)
        return _dot_nt(k_ref[pl.ds(ks, tk), cols], q_ref[:, cols])

    def exact_step(hh, j, carry, masked):
        m, l, acc = carry
        ks = pl.multiple_of(j * tk, tk)
        s = scores(hh, ks)
        if masked:
            s = s + mask_ref[...]
        m_new = jnp.maximum(m, jnp.max(s, axis=0, keepdims=True))
        p = jnp.exp2(s - m_new)
        alpha = jnp.exp2(m - m_new)
        l = alpha * l + jnp.sum(p, axis=0, keepdims=True)
        vt = vt_ref[0, hh * vr:hh * vr + dh, pl.ds(ks, tk)]
        acc = alpha * acc + _dot(vt, p.astype(BF16))
        return m_new, l, acc

    def weighted_values(hh, ks):
        return _dot(vt_ref[0, hh * vr:(hh + 1) * vr, pl.ds(ks, tk)], p_ref[hh])

    def lazy_block(ks, ks_prev, carry, masked):
        out = []
        for hh in range(heads):
            m, acc, over, shift_prev = carry[hh]
            s = scores(hh, ks)
            acc = (acc + weighted_values(hh, ks_prev)) * jnp.exp2(-shift_prev)
            p = jnp.exp2((s + mask_ref[...] if masked else s) - m).astype(BF16)
            p_ref[hh] = p
            over_b = jnp.log2(jnp.max(p, axis=0, keepdims=True).astype(F32))
            shift = jnp.maximum(over_b, 0.0)
            out.append((m + shift, acc, jnp.maximum(over, over_b), shift))
        return tuple(out)

    def finish(hh, l, acc):
        o_ref[0, hh * dh:(hh + 1) * dh, :] = (acc / l).astype(BF16)

    kd = pl.multiple_of(qi * tk, tk)
    ones = jnp.ones((16, LANES), BF16)
    carry = []
    for hh in range(heads):
        cols = slice(hh * LANES, (hh + 1) * LANES)
        own = _dot_nt(ones, q_ref[:, cols] * k_ref[pl.ds(kd, tk), cols])[0:1]
        zero = jnp.zeros((1, tq), F32)
        carry.append((own, jnp.zeros((vr, tq), F32), zero, zero))
    p_ref[...] = jnp.zeros_like(p_ref)

    def blocks(j, n, c):
        for i in range(n):
            prev = jnp.maximum(j + i - 1, 0)
            c = lazy_block(pl.multiple_of((j + i) * tk, tk), pl.multiple_of(prev * tk, tk), c, False)
        return c

    odd = qi % 2
    carry = lax.fori_loop(0, odd, lambda j, c: blocks(j, 1, c), tuple(carry))
    carry = lax.fori_loop(0, qi // 2, lambda t, c: blocks(odd + 2 * t, 2, c), carry)
    carry = lazy_block(kd, pl.multiple_of(jnp.maximum(qi - 1, 0) * tk, tk), carry, True)
    worst = jnp.max(jnp.concatenate([c[2] for c in carry], axis=0))
    for hh in range(heads):
        acc = carry[hh][1] + weighted_values(hh, kd)
        finish(hh, acc[dh:dh + 1], acc[:dh])

    @pl.when(jnp.logical_not(worst <= FOX_LAZY_HEADROOM))
    def _():
        init = (jnp.full((1, tq), -jnp.inf, F32), jnp.zeros((1, tq), F32), jnp.zeros((dh, tq), F32))
        for hh in range(heads):
            c = exact_step(hh, qi, init, True)
            c = lax.fori_loop(0, qi, lambda j, c, hh=hh: exact_step(hh, j, c, False), c)
            finish(hh, c[1], c[2])


def _fox(qaug, kaug, vt, batch, seq):
    tq = min(FOX_TQ, seq)
    tk = min(FOX_TK, tq)
    nq = seq // tq
    hp = FOX_HEADS_PER_STEP
    mask = jnp.asarray(np.where(np.arange(tk)[:, None] <= np.arange(tq)[None, :], 0.0, -np.inf), F32)
    return pl.pallas_call(
        functools.partial(_fox_kernel, tq=tq, tk=tk, heads=hp),
        out_shape=jax.ShapeDtypeStruct((batch, FOX_W, seq), BF16),
        grid=(batch, FOX_HEADS // hp, nq),
        in_specs=[pl.BlockSpec((tq, hp * LANES), lambda b, h, i: (b * nq + i, h)),
                  pl.BlockSpec((seq, hp * LANES), lambda b, h, i: (b, h)),
                  pl.BlockSpec((1, hp * FOX_VROWS, seq), lambda b, h, i: (b, h, 0)),
                  _const_spec((tk, tq))],
        out_specs=pl.BlockSpec((1, hp * FOX_HEAD_DIM, tq), lambda b, h, i: (b, h, i)),
        scratch_shapes=[pltpu.VMEM((hp, tk, tq), BF16)],
        compiler_params=pltpu.CompilerParams(dimension_semantics=("parallel", "parallel", "arbitrary"),
                                             vmem_limit_bytes=VMEM_LIMIT_BYTES),
        name="fox",
    )(qaug, kaug, vt, mask)


def _gla_kernel(gq_ref, gk_ref, gv_ref, gr_ref, la_ref, g_ref, tri_ref, o_ref, st_ref, *, nb, tile):
    @pl.when(pl.program_id(0) == 0)
    def _():
        st_ref[...] = jnp.zeros_like(st_ref)

    c = GLA_CHUNK
    lane_head = lax.broadcasted_iota(jnp.int32, (c, GLA_KW), 1) // GLA_HEAD_K
    st_head = lax.broadcasted_iota(jnp.int32, (GLA_HEAD_V, GLA_KW), 1) // GLA_HEAD_K
    causal = (lax.broadcasted_iota(jnp.int32, (c, c), 0) >= lax.broadcasted_iota(jnp.int32, (c, c), 1))
    tri = tri_ref[...]

    def chunk(ci, _):
        rows = pl.ds(pl.multiple_of(ci * c, c), c)
        bh = [(b, h) for b in range(nb) for h in range(GLA_HEADS)]
        vcols = lambda h: slice(h * GLA_HEAD_V, (h + 1) * GLA_HEAD_V)
        bcs = []
        for b in range(nb):
            la = la_ref[b, rows, :]
            la_hi = la.astype(BF16)
            la_lo = (la - la_hi.astype(F32)).astype(BF16)
            bcs.append(_dot(tri, la_hi) + _dot(tri, la_lo))
        qd, kd, ke, st, st_lo, decay = [], [], [], [], [], []
        for b in range(nb):
            bc = bcs[b]
            bl = bc[c - 1:c, :]
            k = gk_ref[b, rows, :]
            qd.append(gq_ref[b, rows, :] * (GLA_HEAD_K ** -0.5) * jnp.exp(bc))
            kd.append((k * jnp.exp(-bc)).astype(BF16))
            ke.append((k * jnp.exp(bl - bc)).astype(BF16))
            decay.append(jnp.exp(bl))
            st.append(st_ref[b])
            st_lo.append(st[b].astype(BF16))
        qdh = {(b, h): jnp.where(lane_head == h, qd[b], 0.0).astype(BF16) for b, h in bh}
        a = {(b, h): jnp.where(causal, _dot_nt(qdh[b, h], kd[b]), 0.0).astype(BF16) for b, h in bh}
        vh = {(b, h): gv_ref[b, rows, vcols(h)] for b, h in bh}
        o = {(b, h): _dot(a[b, h], vh[b, h]) + _dot_nt(qdh[b, h], st_lo[b]) for b, h in bh}
        u = {(b, h): _dot_tn(vh[b, h], ke[b]) for b, h in bh}
        for b, h in bh:
            gr = gr_ref[b, rows, vcols(h)]
            o_ref[b, rows, vcols(h)] = (_rms_scale(o[b, h]) * g_ref[:, vcols(h)]
                                        * (gr * _sigmoid(gr))).astype(BF16)
        for b in range(nb):
            upd = sum(jnp.where(st_head == h, u[b, h], 0.0) for h in range(GLA_HEADS))
            st_ref[b] = st[b] * decay[b] + upd
        return 0

    lax.fori_loop(0, tile // c, chunk, 0)


def _gla(gq, gk, gv, gr, la, g_gla, batch, seq):
    tile = min(GLA_TILE, seq)
    tri = jnp.asarray(np.tril(np.ones((GLA_CHUNK, GLA_CHUNK), np.float32)), BF16)
    blk = lambda w: pl.BlockSpec((batch, tile, w), lambda i: (0, i, 0))
    r3 = lambda x: x.reshape(batch, seq, x.shape[-1])
    out = pl.pallas_call(
        functools.partial(_gla_kernel, nb=batch, tile=tile),
        out_shape=jax.ShapeDtypeStruct((batch, seq, GLA_VW), BF16),
        grid=(seq // tile,),
        in_specs=[blk(GLA_KW), blk(GLA_KW), blk(GLA_VW), blk(GLA_VW), blk(GLA_KW),
                  _const_spec((1, GLA_VW)), _const_spec((GLA_CHUNK, GLA_CHUNK))],
        out_specs=blk(GLA_VW),
        scratch_shapes=[pltpu.VMEM((batch, GLA_HEAD_V, GLA_KW), F32)],
        compiler_params=pltpu.CompilerParams(dimension_semantics=("arbitrary",),
                                             vmem_limit_bytes=VMEM_LIMIT_BYTES),
        name="gla",
    )(r3(gq), r3(gk), r3(gv), r3(gr), r3(la), g_gla, tri)
    return out.reshape(batch * seq, GLA_VW)


def _memkv_kernel(mem_ref, g_ref, w_ref, k_ref, v_ref):
    mn = (_rms_scale(mem_ref[0]) * g_ref[...]).astype(BF16)
    kv = _dot(mn, w_ref[...])
    k_ref[0] = kv[:, :D_MODEL].astype(BF16)
    v_ref[0] = kv[:, D_MODEL:].astype(BF16)


def _memkv(mem, g, w):
    batch = mem.shape[0]
    blk = pl.BlockSpec((1, N_MEM, D_MODEL), lambda b: (b, 0, 0))
    sds = jax.ShapeDtypeStruct((batch, N_MEM, D_MODEL), BF16)
    return pl.pallas_call(
        _memkv_kernel,
        out_shape=(sds, sds),
        grid=(batch,),
        in_specs=[blk, _const_spec((1, D_MODEL)), _const_spec(w.shape)],
        out_specs=(blk, blk),
        compiler_params=pltpu.CompilerParams(dimension_semantics=("parallel",),
                                             vmem_limit_bytes=VMEM_LIMIT_BYTES),
        name="memkv",
    )(mem, g, w)


def _post_kernel(h_ref, oft_ref, og_ref, wof_ref, wog_ref, g1_ref, g2_ref, wq_ref, km_ref, vm_ref,
                 wo_ref, g3_ref, out_ref):
    m = _dot_tn(oft_ref[0], wof_ref[...]) + _dot(og_ref[...], wog_ref[...])
    h2 = h_ref[...] + _rms_scale(m) * g1_ref[...]
    u = (_rms_scale(h2) * g2_ref[...]).astype(BF16)
    q = _dot(u, wq_ref[...]) * (MEM_HEAD_DIM ** -0.5)
    heads = []
    for hd in range(MEM_HEADS):
        cols = slice(hd * MEM_HEAD_DIM, (hd + 1) * MEM_HEAD_DIM)
        s = _dot_nt(q[:, cols].astype(BF16), km_ref[0, :, cols])
        p = jnp.exp(s - jnp.max(s, axis=-1, keepdims=True))
        p = p / jnp.sum(p, axis=-1, keepdims=True)
        heads.append(_dot(p.astype(BF16), vm_ref[0, :, cols]))
    o = jnp.concatenate(heads, axis=1).astype(BF16)
    c = _dot(o, wo_ref[...])
    out_ref[...] = h2 + _rms_scale(c) * g3_ref[...]


def _post(h, oft, og, wof, wog, g1, g2, wq, km, vm, wo, g3, batch, seq):
    tm = min(TOKEN_TILE, seq)
    nt = seq // tm
    row = lambda w: pl.BlockSpec((tm, w), lambda b, i: (b * nt + i, 0))
    memblk = pl.BlockSpec((1, N_MEM, D_MODEL), lambda b, i: (b, 0, 0))
    g = _const_spec((1, D_MODEL))
    return pl.pallas_call(
        _post_kernel,
        out_shape=jax.ShapeDtypeStruct((batch * seq, D_MODEL), F32),
        grid=(batch, nt),
        in_specs=[row(D_MODEL), pl.BlockSpec((1, FOX_W, tm), lambda b, i: (b, 0, i)), row(GLA_VW),
                  _const_spec(wof.shape), _const_spec(wog.shape), g, g, _const_spec(wq.shape),
                  memblk, memblk, _const_spec(wo.shape), g],
        out_specs=row(D_MODEL),
        compiler_params=pltpu.CompilerParams(dimension_semantics=("parallel", "parallel"),
                                             vmem_limit_bytes=VMEM_LIMIT_BYTES),
        name="post",
    )(h, oft, og, wof, wog, g1, g2, wq, km, vm, wo, g3)


def kernel(x, mem, ffn1_g_pre, ffn1_w_gu, ffn1_w_down, ffn1_g_post, mix_g_pre, mix_w_in, mix_w_a2, mix_b_a,
           mix_b_f, mix_g_gla, mix_w_out, mix_g_post, mem_g_pre, mem_g_kv, mem_w_q, mem_w_kv, mem_w_o,
           mem_g_post, ffn2_g_pre, ffn2_w_gu, ffn2_w_down, ffn2_g_post):
    batch, seq, _ = x.shape
    depth = ffn1_g_pre.shape[0]
    row = lambda v: v.reshape(1, -1)
    h = x.reshape(batch * seq, D_MODEL)
    for l in range(depth):
        wgu1, wd1 = _prep_ffn(ffn1_w_gu[l], ffn1_w_down[l])
        wgu2, wd2 = _prep_ffn(ffn2_w_gu[l], ffn2_w_down[l])
        w1, wvt, wa2, bfp, ba = _prep_mix(mix_w_in[l], mix_w_a2[l], mix_b_a[l], mix_b_f[l])
        w_out = mix_w_out[l].astype(BF16)

        h = _ffn(h, row(ffn1_g_pre[l]), wgu1, wd1, row(ffn1_g_post[l]))
        qaug, kaug, vt, gq, gk, gv, gr, la = _mix_in(h, row(mix_g_pre[l]), w1, wvt, wa2, bfp, ba, batch, seq)
        oft = _fox(qaug, kaug, vt, batch, seq)
        og = _gla(gq, gk, gv, gr, la, row(mix_g_gla[l]), batch, seq)
        km, vm = _memkv(mem, row(mem_g_kv[l]), mem_w_kv[l].astype(BF16))
        h = _post(h, oft, og, w_out[:FOX_W], w_out[FOX_W:], row(mix_g_post[l]), row(mem_g_pre[l]),
                  mem_w_q[l].astype(BF16), km, vm, mem_w_o[l].astype(BF16), row(mem_g_post[l]), batch, seq)
        h = _ffn(h, row(ffn2_g_pre[l]), wgu2, wd2, row(ffn2_g_post[l]))
    return h.reshape(batch, seq, D_MODEL)
```

```python
import functools

import numpy as np
import jax
import jax.numpy as jnp
from jax import lax
from jax.experimental import pallas as pl
from jax.experimental.pallas import tpu as pltpu

F32 = jnp.float32
BF16 = jnp.bfloat16

D_MODEL = 1024
FOX_HEADS = 8
FOX_HEAD_DIM = 64
GLA_HEADS = 4
GLA_HEAD_K = 64
GLA_HEAD_V = 128
GLA_GATE_RANK = 16
GLA_GATE_TAU = 16.0
GLA_CHUNK = 64
FOX_W = FOX_HEADS * FOX_HEAD_DIM
GLA_KW = GLA_HEADS * GLA_HEAD_K
GLA_VW = GLA_HEADS * GLA_HEAD_V
IN_SPLITS = (FOX_W, FOX_W, FOX_W, FOX_HEADS, GLA_KW, GLA_KW, GLA_VW, GLA_VW, GLA_GATE_RANK)
N_MEM = 256
MEM_HEADS = 4
MEM_HEAD_DIM = D_MODEL // MEM_HEADS
D_FF = 2816
MACARON_W = 0.5
RMS_EPS = 1e-6
LOG2E = 1.4426950408889634

LANES = 128
VMEM_LIMIT_BYTES = 56 * 1024 * 1024

TOKEN_TILE = 512
FFN_CHUNK = 256
FOX_TQ = 512
FOX_TK = 512
FOX_HEADS_PER_STEP = 4
FOX_LAZY_HEADROOM = 60.0
FOX_SKIP_MARGIN = 160.0
FOX_NORM_SLACK = 1.02
FOX_VROWS = FOX_HEAD_DIM + 16
GLA_TILE = 512

AUG_HI, AUG_MID, AUG_LO, AUG_ONE = 0, 8, 16, 24
GA_LANE = 8


def _rms_scale(x):
    return x * lax.rsqrt(jnp.mean(x * x, axis=-1, keepdims=True) + RMS_EPS)


def _sigmoid(x):
    return 1.0 / (1.0 + jnp.exp(-x))


def _log_sigmoid(x):
    return jnp.minimum(x, 0.0) - jnp.log(1.0 + jnp.exp(-jnp.abs(x)))


def _dot(a, b):
    return jnp.dot(a, b, preferred_element_type=F32)


def _dot_nt(a, b):
    return lax.dot_general(a, b, (((1,), (1,)), ((), ())), preferred_element_type=F32)


def _dot_tn(a, b):
    return lax.dot_general(a, b, (((0,), (0,)), ((), ())), preferred_element_type=F32)


def _split3(x):
    hi = x.astype(BF16)
    r1 = x - hi.astype(F32)
    mid = r1.astype(BF16)
    lo = (r1 - mid.astype(F32)).astype(BF16)
    return hi, mid, lo


def _const_spec(shape):
    zeros = (0,) * len(shape)
    return pl.BlockSpec(shape, lambda *_: zeros, pipeline_mode=pl.Buffered(1))


def _ffn_math(h, g_pre, wgu_ref, wd_ref, g_post):
    xn = (_rms_scale(h) * g_pre).astype(BF16)
    n_chunks = D_FF // FFN_CHUNK
    acc = None
    for j in range(n_chunks):
        gate = _dot(xn, wgu_ref[:, j * FFN_CHUNK:(j + 1) * FFN_CHUNK])
        up = _dot(xn, wgu_ref[:, D_FF + j * FFN_CHUNK:D_FF + (j + 1) * FFN_CHUNK])
        act = (gate * _sigmoid(gate) * up).astype(BF16)
        part = _dot(act, wd_ref[j * FFN_CHUNK:(j + 1) * FFN_CHUNK, :])
        acc = part if acc is None else acc + part
    return h + MACARON_W * (_rms_scale(acc) * g_post)


def _ffn_kernel(h_ref, gpre_ref, wgu_ref, wd_ref, gpost_ref, o_ref):
    o_ref[...] = _ffn_math(h_ref[...], gpre_ref[...], wgu_ref, wd_ref, gpost_ref[...])


def _prep_ffn(w_gu, w_down):
    return w_gu.astype(BF16), w_down.astype(BF16)


def _ffn(h, g_pre, wgu, wd, g_post):
    t = h.shape[0]
    tm = min(TOKEN_TILE, t)
    row = pl.BlockSpec((tm, D_MODEL), lambda i: (i, 0))
    return pl.pallas_call(
        _ffn_kernel,
        out_shape=jax.ShapeDtypeStruct((t, D_MODEL), F32),
        grid=(t // tm,),
        in_specs=[row, _const_spec((1, D_MODEL)), _const_spec(wgu.shape), _const_spec(wd.shape),
                  _const_spec((1, D_MODEL))],
        out_specs=row,
        compiler_params=pltpu.CompilerParams(dimension_semantics=("parallel",),
                                             vmem_limit_bytes=VMEM_LIMIT_BYTES),
        name="ffn",
    )(h, g_pre, wgu, wd, g_post)


MIX_QW = FOX_HEADS * LANES
MIX_G0 = 2 * FOX_W
MIX_GW = 2 * GLA_KW + 2 * GLA_VW + LANES


def _prep_mix(w_in, w_a2, b_a, b_f):
    offs = np.cumsum(IN_SPLITS)[:-1].tolist()
    fq, fk, fv, fz, gq, gk, gv, gr, ga = jnp.split(w_in, offs, axis=1)
    pad = jnp.zeros((D_MODEL, LANES - FOX_HEADS - GLA_GATE_RANK), F32)
    w1 = jnp.concatenate([fq, fk, gq, gk, gv, gr, fz, ga, pad], axis=1).astype(BF16)
    wvt = fv.T.astype(BF16)
    wa2 = jnp.zeros((LANES, GLA_KW), F32).at[GA_LANE:GA_LANE + GLA_GATE_RANK].set(w_a2).astype(BF16)
    bfp = jnp.zeros((1, LANES), F32).at[0, :FOX_HEADS].set(b_f)
    return w1, wvt, wa2, bfp, b_a.reshape(1, GLA_KW)


def _aug_base(h):
    return FOX_HEAD_DIM if h % 2 == 0 else 0


def _aug_placement():
    p = np.zeros((LANES, 2 * MIX_QW), np.float32)
    for h in range(FOX_HEADS):
        qb = h * LANES + _aug_base(h)
        kb = MIX_QW + h * LANES + _aug_base(h)
        for i, src in enumerate((AUG_HI, AUG_MID, AUG_LO)):
            p[src + h, qb + i] = 1.0
            p[AUG_ONE, kb + i] = 1.0
            p[AUG_ONE, qb + 3 + i] = 1.0
            p[src + h, kb + 3 + i] = -1.0
    return jnp.asarray(p, BF16)


def _mix_in_kernel(h_ref, g_ref, w1_ref, wvt_ref, wa2_ref, bf_ref, ba_ref, p_ref, tri_ref, grp_ref,
                   qaug_ref, kaug_ref, vt_ref, gq_ref, gk_ref, gv_ref, gr_ref, la_ref, stats_ref,
                   carry_ref):
    @pl.when(pl.program_id(1) == 0)
    def _():
        carry_ref[...] = jnp.zeros_like(carry_ref)

    tm = h_ref.shape[0]
    xn = (_rms_scale(h_ref[...]) * g_ref[...]).astype(BF16)
    zq = _dot(xn, w1_ref[:, :FOX_W]) * (FOX_HEAD_DIM ** -0.5 * LOG2E)
    zk = _dot(xn, w1_ref[:, FOX_W:MIX_G0])
    zg = _dot(xn, w1_ref[:, MIX_G0:])
    vt = _dot_nt(wvt_ref[...], xn).astype(BF16)
    for hd in range(FOX_HEADS):
        vt_ref[0, hd * FOX_VROWS:hd * FOX_VROWS + FOX_HEAD_DIM, :] = vt[hd * FOX_HEAD_DIM:(hd + 1) * FOX_HEAD_DIM]
        vt_ref[0, hd * FOX_VROWS + FOX_HEAD_DIM:(hd + 1) * FOX_VROWS, :] = jnp.ones(
            (FOX_VROWS - FOX_HEAD_DIM, tm), BF16)

    o = 0
    gq_ref[...] = zg[:, o:o + GLA_KW]; o += GLA_KW
    gk_ref[...] = zg[:, o:o + GLA_KW]; o += GLA_KW
    gv_ref[...] = zg[:, o:o + GLA_VW].astype(BF16); o += GLA_VW
    gr_ref[...] = zg[:, o:o + GLA_VW]; o += GLA_VW
    small = zg[:, o:o + LANES]

    xa = _dot(small.astype(BF16), wa2_ref[...]) + ba_ref[...]
    la_ref[...] = _log_sigmoid(xa) * (1.0 / GLA_GATE_TAU)

    lane = lax.broadcasted_iota(jnp.int32, (tm, LANES), 1)
    logf = jnp.where(lane < FOX_HEADS, _log_sigmoid(small + bf_ref[...]), 0.0)
    tri = tri_ref[...]
    l_hi, l_mid, l_lo = _split3(logf)
    cum = _dot(tri, l_hi) + _dot(tri, l_mid) + _dot(tri, l_lo) + carry_ref[...]
    carry_ref[...] = cum[tm - 1:tm, :]

    c2 = cum * LOG2E
    qn = jnp.sqrt(jnp.max(_dot((zq * zq).astype(BF16), grp_ref[...]), axis=0, keepdims=True))
    kn = jnp.sqrt(jnp.max(_dot((zk * zk).astype(BF16), grp_ref[...]), axis=0, keepdims=True))
    stats_ref[0] = jnp.concatenate(
        [qn, kn, c2[0:1], c2[tm - 1:tm], jnp.zeros((4, LANES), F32)], axis=0)

    c_hi, c_mid, c_lo = _split3(c2)
    packed = (c_hi.astype(F32) + pltpu.roll(c_mid.astype(F32), AUG_MID, 1)
              + pltpu.roll(c_lo.astype(F32), AUG_LO, 1))
    packed = jnp.where(lane == AUG_ONE, 1.0, packed)
    aug = _dot(packed.astype(BF16), p_ref[...])
    low_half = lane < FOX_HEAD_DIM
    for hd in range(FOX_HEADS):
        pair = slice((hd // 2) * LANES, (hd // 2 + 1) * LANES)
        cols = slice(hd * LANES, (hd + 1) * LANES)
        own = low_half if hd % 2 == 0 else jnp.logical_not(low_half)
        qaug_ref[:, cols] = (jnp.where(own, zq[:, pair], 0.0) + aug[:, cols]).astype(BF16)
        kaug_ref[:, cols] = (jnp.where(own, zk[:, pair], 0.0)
                             + aug[:, MIX_QW + hd * LANES:MIX_QW + (hd + 1) * LANES]).astype(BF16)


def _mix_in(h, g, w1, wvt, wa2, bfp, ba, batch, seq):
    tm = min(TOKEN_TILE, seq)
    nt = seq // tm
    tri = jnp.asarray(np.tril(np.ones((tm, tm), np.float32)), BF16)
    grp = jnp.asarray(np.arange(FOX_W)[:, None] // FOX_HEAD_DIM == np.arange(LANES)[None, :], BF16)
    row = lambda w: pl.BlockSpec((tm, w), lambda b, i: (b * nt + i, 0))
    t = batch * seq
    sds = jax.ShapeDtypeStruct
    return pl.pallas_call(
        _mix_in_kernel,
        out_shape=(sds((t, MIX_QW), BF16), sds((t, MIX_QW), BF16),
                   sds((batch, FOX_HEADS * FOX_VROWS, seq), BF16),
                   sds((t, GLA_KW), F32), sds((t, GLA_KW), F32), sds((t, GLA_VW), BF16),
                   sds((t, GLA_VW), F32), sds((t, GLA_KW), F32), sds((batch * nt, 8, LANES), F32)),
        grid=(batch, nt),
        in_specs=[row(D_MODEL), _const_spec((1, D_MODEL)), _const_spec(w1.shape), _const_spec(wvt.shape),
                  _const_spec(wa2.shape), _const_spec(bfp.shape), _const_spec(ba.shape),
                  _const_spec((LANES, 2 * MIX_QW)), _const_spec((tm, tm)), _const_spec((FOX_W, LANES))],
        out_specs=(row(MIX_QW), row(MIX_QW),
                   pl.BlockSpec((1, FOX_HEADS * FOX_VROWS, tm), lambda b, i: (b, 0, i)),
                   row(GLA_KW), row(GLA_KW), row(GLA_VW), row(GLA_VW), row(GLA_KW),
                   pl.BlockSpec((1, 8, LANES), lambda b, i: (b * nt + i, 0, 0))),
        scratch_shapes=[pltpu.VMEM((1, LANES), F32)],
        compiler_params=pltpu.CompilerParams(dimension_semantics=("arbitrary", "arbitrary"),
                                             vmem_limit_bytes=VMEM_LIMIT_BYTES),
        name="mix_in",
    )(h, g, w1, wvt, wa2, bfp, ba, _aug_placement(), tri, grp)


def _fox_kernel(stats_ref, q_ref, k_ref, vt_ref, mask_ref, o_ref, p_ref, *, tq, tk, heads, nt):
    qi = pl.program_id(2)
    dh = FOX_HEAD_DIM
    vr = FOX_VROWS
    assert tq == tk

    def first_needed(hh):
        base = ((pl.program_id(0) * FOX_HEADS + pl.program_id(1) * heads + hh) * 4) * nt
        qn = stats_ref[base + qi] * FOX_NORM_SLACK
        kn_own = stats_ref[base + nt + qi] * FOX_NORM_SLACK
        kn_max = lax.fori_loop(0, qi + 1, lambda j, a: jnp.maximum(a, stats_ref[base + nt + j]),
                               jnp.float32(0.0)) * FOX_NORM_SLACK
        reach = qn * (kn_max + kn_own) + stats_ref[base + 2 * nt + qi] + FOX_SKIP_MARGIN
        return lax.fori_loop(
            0, qi, lambda j, n: n + (reach < stats_ref[base + 3 * nt + j]).astype(jnp.int32),
            jnp.int32(0))

    def scores(hh, ks):
        cols = slice(hh * LANES, (hh + 1) * LANES)
        return _dot_nt(k_ref[pl.ds(ks, tk), cols], q_ref[:, cols])

    def exact_step(hh, j, carry, masked):
        m, l, acc = carry
        ks = pl.multiple_of(j * tk, tk)
        s = scores(hh, ks)
        if masked:
            s = s + mask_ref[...]
        m_new = jnp.maximum(m, jnp.max(s, axis=0, keepdims=True))
        p = jnp.exp2(s - m_new)
        alpha = jnp.exp2(m - m_new)
        l = alpha * l + jnp.sum(p, axis=0, keepdims=True)
        vt = vt_ref[0, hh * vr:hh * vr + dh, pl.ds(ks, tk)]
        acc = alpha * acc + _dot(vt, p.astype(BF16))
        return m_new, l, acc

    def weighted_values(hh, ks):
        return _dot(vt_ref[0, hh * vr:(hh + 1) * vr, pl.ds(ks, tk)], p_ref[hh])

    def lazy_block(ks, ks_prev, carry, masked):
        out = []
        for hh in range(heads):
            m, acc, over, shift_prev = carry[hh]
            s = scores(hh, ks)
            acc = (acc + weighted_values(hh, ks_prev)) * jnp.exp2(-shift_prev)
            p = jnp.exp2((s + mask_ref[...] if masked else s) - m).astype(BF16)
            p_ref[hh] = p
            over_b = jnp.log2(jnp.max(p, axis=0, keepdims=True).astype(F32))
            shift = jnp.maximum(over_b, 0.0)
            out.append((m + shift, acc, jnp.maximum(over, over_b), shift))
        return tuple(out)

    def finish(hh, l, acc):
        o_ref[0, hh * dh:(hh + 1) * dh, :] = (acc / l).astype(BF16)

    kd = pl.multiple_of(qi * tk, tk)
    ones = jnp.ones((16, LANES), BF16)
    carry = []
    for hh in range(heads):
        cols = slice(hh * LANES, (hh + 1) * LANES)
        own = _dot_nt(ones, q_ref[:, cols] * k_ref[pl.ds(kd, tk), cols])[0:1]
        zero = jnp.zeros((1, tq), F32)
        carry.append((own, jnp.zeros((vr, tq), F32), zero, zero))
    p_ref[...] = jnp.zeros_like(p_ref)

    def blocks(j, n, c):
        for i in range(n):
            prev = jnp.maximum(j + i - 1, 0)
            c = lazy_block(pl.multiple_of((j + i) * tk, tk), pl.multiple_of(prev * tk, tk), c, False)
        return c

    j0 = first_needed(0)
    for hh in range(1, heads):
        j0 = jnp.minimum(j0, first_needed(hh))
    odd = (qi - j0) % 2
    carry = lax.fori_loop(0, odd, lambda j, c: blocks(j0 + j, 1, c), tuple(carry))
    carry = lax.fori_loop(0, (qi - j0) // 2, lambda t, c: blocks(j0 + odd + 2 * t, 2, c), carry)
    carry = lazy_block(kd, pl.multiple_of(jnp.maximum(qi - 1, 0) * tk, tk), carry, True)
    worst = jnp.max(jnp.concatenate([c[2] for c in carry], axis=0))
    for hh in range(heads):
        acc = carry[hh][1] + weighted_values(hh, kd)
        finish(hh, acc[dh:dh + 1], acc[:dh])

    @pl.when(jnp.logical_not(worst <= FOX_LAZY_HEADROOM))
    def _():
        init = (jnp.full((1, tq), -jnp.inf, F32), jnp.zeros((1, tq), F32), jnp.zeros((dh, tq), F32))
        for hh in range(heads):
            c = exact_step(hh, qi, init, True)
            c = lax.fori_loop(0, qi, lambda j, c, hh=hh: exact_step(hh, j, c, False), c)
            finish(hh, c[1], c[2])


def _fox(qaug, kaug, vt, stats, batch, seq):
    tq = min(FOX_TQ, seq)
    tk = min(FOX_TK, tq)
    nq = seq // tq
    hp = FOX_HEADS_PER_STEP
    assert tk == min(TOKEN_TILE, seq)
    mask = jnp.asarray(np.where(np.arange(tk)[:, None] <= np.arange(tq)[None, :], 0.0, -np.inf), F32)
    table = stats[:, :4, :FOX_HEADS].reshape(batch, nq, 4, FOX_HEADS).transpose(0, 3, 2, 1).reshape(-1)
    tile_map = lambda f: (lambda b, h, i, tbl: f(b, h, i))
    return pl.pallas_call(
        functools.partial(_fox_kernel, tq=tq, tk=tk, heads=hp, nt=nq),
        out_shape=jax.ShapeDtypeStruct((batch, FOX_W, seq), BF16),
        grid_spec=pltpu.PrefetchScalarGridSpec(
            num_scalar_prefetch=1,
            grid=(batch, FOX_HEADS // hp, nq),
            in_specs=[pl.BlockSpec((tq, hp * LANES), tile_map(lambda b, h, i: (b * nq + i, h))),
                      pl.BlockSpec((seq, hp * LANES), tile_map(lambda b, h, i: (b, h))),
                      pl.BlockSpec((1, hp * FOX_VROWS, seq), tile_map(lambda b, h, i: (b, h, 0))),
                      pl.BlockSpec((tk, tq), tile_map(lambda b, h, i: (0, 0)),
                                   pipeline_mode=pl.Buffered(1))],
            out_specs=pl.BlockSpec((1, hp * FOX_HEAD_DIM, tq), tile_map(lambda b, h, i: (b, h, i))),
            scratch_shapes=[pltpu.VMEM((hp, tk, tq), BF16)]),
        compiler_params=pltpu.CompilerParams(dimension_semantics=("parallel", "parallel", "arbitrary"),
                                             vmem_limit_bytes=VMEM_LIMIT_BYTES),
        name="fox",
    )(table, qaug, kaug, vt, mask)


def _gla_kernel(gq_ref, gk_ref, gv_ref, gr_ref, la_ref, g_ref, tri_ref, o_ref, st_ref, *, nb, tile):
    @pl.when(pl.program_id(0) == 0)
    def _():
        st_ref[...] = jnp.zeros_like(st_ref)

    c = GLA_CHUNK
    lane_head = lax.broadcasted_iota(jnp.int32, (c, GLA_KW), 1) // GLA_HEAD_K
    st_head = lax.broadcasted_iota(jnp.int32, (GLA_HEAD_V, GLA_KW), 1) // GLA_HEAD_K
    causal = (lax.broadcasted_iota(jnp.int32, (c, c), 0) >= lax.broadcasted_iota(jnp.int32, (c, c), 1))
    tri = tri_ref[...]

    def chunk(ci, _):
        rows = pl.ds(pl.multiple_of(ci * c, c), c)
        bh = [(b, h) for b in range(nb) for h in range(GLA_HEADS)]
        vcols = lambda h: slice(h * GLA_HEAD_V, (h + 1) * GLA_HEAD_V)
        bcs = []
        for b in range(nb):
            la = la_ref[b, rows, :]
            la_hi = la.astype(BF16)
            la_lo = (la - la_hi.astype(F32)).astype(BF16)
            bcs.append(_dot(tri, la_hi) + _dot(tri, la_lo))
        qd, kd, ke, st, st_lo, decay = [], [], [], [], [], []
        for b in range(nb):
            bc = bcs[b]
            bl = bc[c - 1:c, :]
            k = gk_ref[b, rows, :]
            qd.append(gq_ref[b, rows, :] * (GLA_HEAD_K ** -0.5) * jnp.exp(bc))
            kd.append((k * jnp.exp(-bc)).astype(BF16))
            ke.append((k * jnp.exp(bl - bc)).astype(BF16))
            decay.append(jnp.exp(bl))
            st.append(st_ref[b])
            st_lo.append(st[b].astype(BF16))
        qdh = {(b, h): jnp.where(lane_head == h, qd[b], 0.0).astype(BF16) for b, h in bh}
        a = {(b, h): jnp.where(causal, _dot_nt(qdh[b, h], kd[b]), 0.0).astype(BF16) for b, h in bh}
        vh = {(b, h): gv_ref[b, rows, vcols(h)] for b, h in bh}
        o = {(b, h): _dot(a[b, h], vh[b, h]) + _dot_nt(qdh[b, h], st_lo[b]) for b, h in bh}
        u = {(b, h): _dot_tn(vh[b, h], ke[b]) for b, h in bh}
        for b, h in bh:
            gr = gr_ref[b, rows, vcols(h)]
            o_ref[b, rows, vcols(h)] = (_rms_scale(o[b, h]) * g_ref[:, vcols(h)]
                                        * (gr * _sigmoid(gr))).astype(BF16)
        for b in range(nb):
            upd = sum(jnp.where(st_head == h, u[b, h], 0.0) for h in range(GLA_HEADS))
            st_ref[b] = st[b] * decay[b] + upd
        return 0

    lax.fori_loop(0, tile // c, chunk, 0)


def _gla(gq, gk, gv, gr, la, g_gla, batch, seq):
    tile = min(GLA_TILE, seq)
    tri = jnp.asarray(np.tril(np.ones((GLA_CHUNK, GLA_CHUNK), np.float32)), BF16)
    blk = lambda w: pl.BlockSpec((batch, tile, w), lambda i: (0, i, 0))
    r3 = lambda x: x.reshape(batch, seq, x.shape[-1])
    out = pl.pallas_call(
        functools.partial(_gla_kernel, nb=batch, tile=tile),
        out_shape=jax.ShapeDtypeStruct((batch, seq, GLA_VW), BF16),
        grid=(seq // tile,),
        in_specs=[blk(GLA_KW), blk(GLA_KW), blk(GLA_VW), blk(GLA_VW), blk(GLA_KW),
                  _const_spec((1, GLA_VW)), _const_spec((GLA_CHUNK, GLA_CHUNK))],
        out_specs=blk(GLA_VW),
        scratch_shapes=[pltpu.VMEM((batch, GLA_HEAD_V, GLA_KW), F32)],
        compiler_params=pltpu.CompilerParams(dimension_semantics=("arbitrary",),
                                             vmem_limit_bytes=VMEM_LIMIT_BYTES),
        name="gla",
    )(r3(gq), r3(gk), r3(gv), r3(gr), r3(la), g_gla, tri)
    return out.reshape(batch * seq, GLA_VW)


def _memkv_kernel(mem_ref, g_ref, w_ref, k_ref, v_ref):
    mn = (_rms_scale(mem_ref[0]) * g_ref[...]).astype(BF16)
    kv = _dot(mn, w_ref[...])
    k_ref[0] = kv[:, :D_MODEL].astype(BF16)
    v_ref[0] = kv[:, D_MODEL:].astype(BF16)


def _memkv(mem, g, w):
    batch = mem.shape[0]
    blk = pl.BlockSpec((1, N_MEM, D_MODEL), lambda b: (b, 0, 0))
    sds = jax.ShapeDtypeStruct((batch, N_MEM, D_MODEL), BF16)
    return pl.pallas_call(
        _memkv_kernel,
        out_shape=(sds, sds),
        grid=(batch,),
        in_specs=[blk, _const_spec((1, D_MODEL)), _const_spec(w.shape)],
        out_specs=(blk, blk),
        compiler_params=pltpu.CompilerParams(dimension_semantics=("parallel",),
                                             vmem_limit_bytes=VMEM_LIMIT_BYTES),
        name="memkv",
    )(mem, g, w)


def _post_kernel(h_ref, oft_ref, og_ref, wof_ref, wog_ref, g1_ref, g2_ref, wq_ref, km_ref, vm_ref,
                 wo_ref, g3_ref, out_ref):
    m = _dot_tn(oft_ref[0], wof_ref[...]) + _dot(og_ref[...], wog_ref[...])
    h2 = h_ref[...] + _rms_scale(m) * g1_ref[...]
    u = (_rms_scale(h2) * g2_ref[...]).astype(BF16)
    q = _dot(u, wq_ref[...]) * (MEM_HEAD_DIM ** -0.5)
    heads = []
    for hd in range(MEM_HEADS):
        cols = slice(hd * MEM_HEAD_DIM, (hd + 1) * MEM_HEAD_DIM)
        s = _dot_nt(q[:, cols].astype(BF16), km_ref[0, :, cols])
        p = jnp.exp(s - jnp.max(s, axis=-1, keepdims=True))
        p = p / jnp.sum(p, axis=-1, keepdims=True)
        heads.append(_dot(p.astype(BF16), vm_ref[0, :, cols]))
    o = jnp.concatenate(heads, axis=1).astype(BF16)
    c = _dot(o, wo_ref[...])
    out_ref[...] = h2 + _rms_scale(c) * g3_ref[...]


def _post(h, oft, og, wof, wog, g1, g2, wq, km, vm, wo, g3, batch, seq):
    tm = min(TOKEN_TILE, seq)
    nt = seq // tm
    row = lambda w: pl.BlockSpec((tm, w), lambda b, i: (b * nt + i, 0))
    memblk = pl.BlockSpec((1, N_MEM, D_MODEL), lambda b, i: (b, 0, 0))
    g = _const_spec((1, D_MODEL))
    return pl.pallas_call(
        _post_kernel,
        out_shape=jax.ShapeDtypeStruct((batch * seq, D_MODEL), F32),
        grid=(batch, nt),
        in_specs=[row(D_MODEL), pl.BlockSpec((1, FOX_W, tm), lambda b, i: (b, 0, i)), row(GLA_VW),
                  _const_spec(wof.shape), _const_spec(wog.shape), g, g, _const_spec(wq.shape),
                  memblk, memblk, _const_spec(wo.shape), g],
        out_specs=row(D_MODEL),
        compiler_params=pltpu.CompilerParams(dimension_semantics=("parallel", "parallel"),
                                             vmem_limit_bytes=VMEM_LIMIT_BYTES),
        name="post",
    )(h, oft, og, wof, wog, g1, g2, wq, km, vm, wo, g3)


def kernel(x, mem, ffn1_g_pre, ffn1_w_gu, ffn1_w_down, ffn1_g_post, mix_g_pre, mix_w_in, mix_w_a2, mix_b_a,
           mix_b_f, mix_g_gla, mix_w_out, mix_g_post, mem_g_pre, mem_g_kv, mem_w_q, mem_w_kv, mem_w_o,
           mem_g_post, ffn2_g_pre, ffn2_w_gu, ffn2_w_down, ffn2_g_post):
    batch, seq, _ = x.shape
    depth = ffn1_g_pre.shape[0]
    row = lambda v: v.reshape(1, -1)
    h = x.reshape(batch * seq, D_MODEL)
    for l in range(depth):
        wgu1, wd1 = _prep_ffn(ffn1_w_gu[l], ffn1_w_down[l])
        wgu2, wd2 = _prep_ffn(ffn2_w_gu[l], ffn2_w_down[l])
        w1, wvt, wa2, bfp, ba = _prep_mix(mix_w_in[l], mix_w_a2[l], mix_b_a[l], mix_b_f[l])
        w_out = mix_w_out[l].astype(BF16)

        h = _ffn(h, row(ffn1_g_pre[l]), wgu1, wd1, row(ffn1_g_post[l]))
        qaug, kaug, vt, gq, gk, gv, gr, la, stats = _mix_in(h, row(mix_g_pre[l]), w1, wvt, wa2, bfp, ba,
                                                            batch, seq)
        oft = _fox(qaug, kaug, vt, stats, batch, seq)
        og = _gla(gq, gk, gv, gr, la, row(mix_g_gla[l]), batch, seq)
        km, vm = _memkv(mem, row(mem_g_kv[l]), mem_w_kv[l].astype(BF16))
        h = _post(h, oft, og, w_out[:FOX_W], w_out[FOX_W:], row(mix_g_post[l]), row(mem_g_pre[l]),
                  mem_w_q[l].astype(BF16), km, vm, mem_w_o[l].astype(BF16), row(mem_g_post[l]), batch, seq)
        h = _ffn(h, row(ffn2_g_pre[l]), wgu2, wd2, row(ffn2_g_post[l]))
    return h.reshape(batch, seq, D_MODEL)
```

```python
import functools

import numpy as np
import jax
import jax.numpy as jnp
from jax import lax
from jax.experimental import pallas as pl
from jax.experimental.pallas import tpu as pltpu

F32 = jnp.float32
BF16 = jnp.bfloat16

D_MODEL = 1024
FOX_HEADS = 8
FOX_HEAD_DIM = 64
GLA_HEADS = 4
GLA_HEAD_K = 64
GLA_HEAD_V = 128
GLA_GATE_RANK = 16
GLA_GATE_TAU = 16.0
GLA_CHUNK = 64
FOX_W = FOX_HEADS * FOX_HEAD_DIM
GLA_KW = GLA_HEADS * GLA_HEAD_K
GLA_VW = GLA_HEADS * GLA_HEAD_V
IN_SPLITS = (FOX_W, FOX_W, FOX_W, FOX_HEADS, GLA_KW, GLA_KW, GLA_VW, GLA_VW, GLA_GATE_RANK)
N_MEM = 256
MEM_HEADS = 4
MEM_HEAD_DIM = D_MODEL // MEM_HEADS
D_FF = 2816
MACARON_W = 0.5
RMS_EPS = 1e-6
LOG2E = 1.4426950408889634

LANES = 128
VMEM_LIMIT_BYTES = 56 * 1024 * 1024

TOKEN_TILE = 512
FFN_CHUNK = 256
FOX_TQ = 256
FOX_TK = 256
FOX_HEADS_PER_STEP = 4
FOX_LAZY_HEADROOM = 60.0
FOX_SKIP_MARGIN = 160.0
FOX_NORM_SLACK = 1.02
FOX_VROWS = FOX_HEAD_DIM + 16
GLA_TILE = 512

AUG_HI, AUG_MID, AUG_LO, AUG_ONE = 0, 8, 16, 24
GA_LANE = 8


def _rms_scale(x):
    return x * lax.rsqrt(jnp.mean(x * x, axis=-1, keepdims=True) + RMS_EPS)


def _sigmoid(x):
    return 1.0 / (1.0 + jnp.exp(-x))


def _log_sigmoid(x):
    return jnp.minimum(x, 0.0) - jnp.log(1.0 + jnp.exp(-jnp.abs(x)))


def _dot(a, b):
    return jnp.dot(a, b, preferred_element_type=F32)


def _dot_nt(a, b):
    return lax.dot_general(a, b, (((1,), (1,)), ((), ())), preferred_element_type=F32)


def _dot_tn(a, b):
    return lax.dot_general(a, b, (((0,), (0,)), ((), ())), preferred_element_type=F32)


def _split3(x):
    hi = x.astype(BF16)
    r1 = x - hi.astype(F32)
    mid = r1.astype(BF16)
    lo = (r1 - mid.astype(F32)).astype(BF16)
    return hi, mid, lo


def _const_spec(shape):
    zeros = (0,) * len(shape)
    return pl.BlockSpec(shape, lambda *_: zeros, pipeline_mode=pl.Buffered(1))


def _ffn_math(h, g_pre, wgu_ref, wd_ref, g_post):
    xn = (_rms_scale(h) * g_pre).astype(BF16)
    n_chunks = D_FF // FFN_CHUNK
    acc = None
    for j in range(n_chunks):
        gate = _dot(xn, wgu_ref[:, j * FFN_CHUNK:(j + 1) * FFN_CHUNK])
        up = _dot(xn, wgu_ref[:, D_FF + j * FFN_CHUNK:D_FF + (j + 1) * FFN_CHUNK])
        act = (gate * _sigmoid(gate) * up).astype(BF16)
        part = _dot(act, wd_ref[j * FFN_CHUNK:(j + 1) * FFN_CHUNK, :])
        acc = part if acc is None else acc + part
    return h + MACARON_W * (_rms_scale(acc) * g_post)


def _ffn_kernel(h_ref, gpre_ref, wgu_ref, wd_ref, gpost_ref, o_ref):
    o_ref[...] = _ffn_math(h_ref[...], gpre_ref[...], wgu_ref, wd_ref, gpost_ref[...])


def _prep_ffn(w_gu, w_down):
    return w_gu.astype(BF16), w_down.astype(BF16)


def _ffn(h, g_pre, wgu, wd, g_post):
    t = h.shape[0]
    tm = min(TOKEN_TILE, t)
    row = pl.BlockSpec((tm, D_MODEL), lambda i: (i, 0))
    return pl.pallas_call(
        _ffn_kernel,
        out_shape=jax.ShapeDtypeStruct((t, D_MODEL), F32),
        grid=(t // tm,),
        in_specs=[row, _const_spec((1, D_MODEL)), _const_spec(wgu.shape), _const_spec(wd.shape),
                  _const_spec((1, D_MODEL))],
        out_specs=row,
        compiler_params=pltpu.CompilerParams(dimension_semantics=("parallel",),
                                             vmem_limit_bytes=VMEM_LIMIT_BYTES),
        name="ffn",
    )(h, g_pre, wgu, wd, g_post)


MIX_QW = FOX_HEADS * LANES
MIX_G0 = 2 * FOX_W
MIX_GW = 2 * GLA_KW + 2 * GLA_VW + LANES


def _prep_mix(w_in, w_a2, b_a, b_f):
    offs = np.cumsum(IN_SPLITS)[:-1].tolist()
    fq, fk, fv, fz, gq, gk, gv, gr, ga = jnp.split(w_in, offs, axis=1)
    pad = jnp.zeros((D_MODEL, LANES - FOX_HEADS - GLA_GATE_RANK), F32)
    w1 = jnp.concatenate([fq, fk, gq, gk, gv, gr, fz, ga, pad], axis=1).astype(BF16)
    wvt = fv.T.astype(BF16)
    wa2 = jnp.zeros((LANES, GLA_KW), F32).at[GA_LANE:GA_LANE + GLA_GATE_RANK].set(w_a2).astype(BF16)
    bfp = jnp.zeros((1, LANES), F32).at[0, :FOX_HEADS].set(b_f)
    return w1, wvt, wa2, bfp, b_a.reshape(1, GLA_KW)


def _aug_base(h):
    return FOX_HEAD_DIM if h % 2 == 0 else 0


def _aug_placement():
    p = np.zeros((LANES, 2 * MIX_QW), np.float32)
    for h in range(FOX_HEADS):
        qb = h * LANES + _aug_base(h)
        kb = MIX_QW + h * LANES + _aug_base(h)
        for i, src in enumerate((AUG_HI, AUG_MID, AUG_LO)):
            p[src + h, qb + i] = 1.0
            p[AUG_ONE, kb + i] = 1.0
            p[AUG_ONE, qb + 3 + i] = 1.0
            p[src + h, kb + 3 + i] = -1.0
    return jnp.asarray(p, BF16)


def _mix_in_kernel(h_ref, g_ref, w1_ref, wvt_ref, wa2_ref, bf_ref, ba_ref, p_ref, tri_ref, grp_ref, grpt_ref,
                   qaug_ref, kaug_ref, vt_ref, gq_ref, gk_ref, gv_ref, gr_ref, la_ref, own_ref, skip_ref,
                   carry_ref, hist_ref, *, fox_tk):
    @pl.when(pl.program_id(1) == 0)
    def _():
        carry_ref[...] = jnp.zeros_like(carry_ref)
        hist_ref[...] = jnp.full(hist_ref.shape, -jnp.inf, F32)

    tm = h_ref.shape[0]
    xn = (_rms_scale(h_ref[...]) * g_ref[...]).astype(BF16)
    zq = _dot(xn, w1_ref[:, :FOX_W]) * (FOX_HEAD_DIM ** -0.5 * LOG2E)
    zk = _dot(xn, w1_ref[:, FOX_W:MIX_G0])
    zg = _dot(xn, w1_ref[:, MIX_G0:])
    vt = _dot_nt(wvt_ref[...], xn).astype(BF16)
    for hd in range(FOX_HEADS):
        vt_ref[0, hd * FOX_VROWS:hd * FOX_VROWS + FOX_HEAD_DIM, :] = vt[hd * FOX_HEAD_DIM:(hd + 1) * FOX_HEAD_DIM]
        vt_ref[0, hd * FOX_VROWS + FOX_HEAD_DIM:(hd + 1) * FOX_VROWS, :] = jnp.ones(
            (FOX_VROWS - FOX_HEAD_DIM, tm), BF16)

    o = 0
    gq_ref[...] = zg[:, o:o + GLA_KW]; o += GLA_KW
    gk_ref[...] = zg[:, o:o + GLA_KW]; o += GLA_KW
    gv_ref[...] = zg[:, o:o + GLA_VW].astype(BF16); o += GLA_VW
    gr_ref[...] = zg[:, o:o + GLA_VW]; o += GLA_VW
    small = zg[:, o:o + LANES]

    xa = _dot(small.astype(BF16), wa2_ref[...]) + ba_ref[...]
    la_ref[...] = _log_sigmoid(xa) * (1.0 / GLA_GATE_TAU)

    lane = lax.broadcasted_iota(jnp.int32, (tm, LANES), 1)
    logf = jnp.where(lane < FOX_HEADS, _log_sigmoid(small + bf_ref[...]), 0.0)
    tri = tri_ref[...]
    l_hi, l_mid, l_lo = _split3(logf)
    cum = _dot(tri, l_hi) + _dot(tri, l_mid) + _dot(tri, l_lo) + carry_ref[0:1, :]
    carry_ref[0:1, :] = cum[tm - 1:tm, :]

    c2 = cum * LOG2E
    own_ref[0] = _dot_nt(grpt_ref[...], (zq * zk).astype(BF16))[:FOX_HEADS]

    qn2 = _dot((zq * zq).astype(BF16), grp_ref[...])
    kn2 = _dot((zk * zk).astype(BF16), grp_ref[...])
    nsub = tm // fox_tk
    rows = []
    kn_run = carry_ref[1:2, :]
    for sb in range(nsub):
        r0 = sb * fox_tk
        kn_run = jnp.maximum(kn_run, jnp.sqrt(jnp.max(kn2[r0:r0 + fox_tk], axis=0, keepdims=True)))
        qn = jnp.sqrt(jnp.max(qn2[r0:r0 + fox_tk], axis=0, keepdims=True))
        reach = 2.0 * (FOX_NORM_SLACK * FOX_NORM_SLACK) * qn * kn_run + c2[r0:r0 + 1] + FOX_SKIP_MARGIN
        rows.append(jnp.sum(jnp.where(reach < hist_ref[...], 1.0, 0.0), axis=0, keepdims=True))
        hist_ref[pl.ds(pl.program_id(1) * nsub + sb, 1), :] = c2[r0 + fox_tk - 1:r0 + fox_tk]
    carry_ref[1:2, :] = kn_run
    skip_ref[0] = jnp.concatenate(rows + [jnp.zeros((8 - nsub, LANES), F32)], axis=0)

    c_hi, c_mid, c_lo = _split3(c2)
    packed = (c_hi.astype(F32) + pltpu.roll(c_mid.astype(F32), AUG_MID, 1)
              + pltpu.roll(c_lo.astype(F32), AUG_LO, 1))
    packed = jnp.where(lane == AUG_ONE, 1.0, packed)
    aug = _dot(packed.astype(BF16), p_ref[...])
    low_half = lane < FOX_HEAD_DIM
    for hd in range(FOX_HEADS):
        pair = slice((hd // 2) * LANES, (hd // 2 + 1) * LANES)
        cols = slice(hd * LANES, (hd + 1) * LANES)
        own = low_half if hd % 2 == 0 else jnp.logical_not(low_half)
        qaug_ref[:, cols] = (jnp.where(own, zq[:, pair], 0.0) + aug[:, cols]).astype(BF16)
        kaug_ref[:, cols] = (jnp.where(own, zk[:, pair], 0.0)
                             + aug[:, MIX_QW + hd * LANES:MIX_QW + (hd + 1) * LANES]).astype(BF16)


def _mix_in(h, g, w1, wvt, wa2, bfp, ba, batch, seq):
    tm = min(TOKEN_TILE, seq)
    nt = seq // tm
    tri = jnp.asarray(np.tril(np.ones((tm, tm), np.float32)), BF16)
    grp_np = np.arange(FOX_W)[:, None] // FOX_HEAD_DIM == np.arange(LANES)[None, :]
    grp = jnp.asarray(grp_np, BF16)
    grpt = jnp.asarray(grp_np[:, :16].T, BF16)
    fox_tk = min(FOX_TK, seq)
    row = lambda w: pl.BlockSpec((tm, w), lambda b, i: (b * nt + i, 0))
    t = batch * seq
    sds = jax.ShapeDtypeStruct
    return pl.pallas_call(
        functools.partial(_mix_in_kernel, fox_tk=fox_tk),
        out_shape=(sds((t, MIX_QW), BF16), sds((t, MIX_QW), BF16),
                   sds((batch, FOX_HEADS * FOX_VROWS, seq), BF16),
                   sds((t, GLA_KW), F32), sds((t, GLA_KW), F32), sds((t, GLA_VW), BF16),
                   sds((t, GLA_VW), F32), sds((t, GLA_KW), F32), sds((batch, FOX_HEADS, seq), F32),
                   sds((batch * nt, 8, LANES), F32)),
        grid=(batch, nt),
        in_specs=[row(D_MODEL), _const_spec((1, D_MODEL)), _const_spec(w1.shape), _const_spec(wvt.shape),
                  _const_spec(wa2.shape), _const_spec(bfp.shape), _const_spec(ba.shape),
                  _const_spec((LANES, 2 * MIX_QW)), _const_spec((tm, tm)), _const_spec((FOX_W, LANES)),
                  _const_spec((16, FOX_W))],
        out_specs=(row(MIX_QW), row(MIX_QW),
                   pl.BlockSpec((1, FOX_HEADS * FOX_VROWS, tm), lambda b, i: (b, 0, i)),
                   row(GLA_KW), row(GLA_KW), row(GLA_VW), row(GLA_VW), row(GLA_KW),
                   pl.BlockSpec((1, FOX_HEADS, tm), lambda b, i: (b, 0, i)),
                   pl.BlockSpec((1, 8, LANES), lambda b, i: (b * nt + i, 0, 0))),
        scratch_shapes=[pltpu.VMEM((2, LANES), F32), pltpu.VMEM((seq // fox_tk, LANES), F32)],
        compiler_params=pltpu.CompilerParams(dimension_semantics=("arbitrary", "arbitrary"),
                                             vmem_limit_bytes=VMEM_LIMIT_BYTES),
        name="mix_in",
    )(h, g, w1, wvt, wa2, bfp, ba, _aug_placement(), tri, grp, grpt)


def _fox_kernel(skip_ref, q_ref, k_ref, vt_ref, own_ref, mask_ref, o_ref, p_ref, *, tq, tk, heads, nt):
    qi = pl.program_id(2)
    dh = FOX_HEAD_DIM
    vr = FOX_VROWS
    assert tq == tk

    def scores(hh, ks):
        cols = slice(hh * LANES, (hh + 1) * LANES)
        return _dot_nt(k_ref[pl.ds(ks, tk), cols], q_ref[:, cols])

    def exact_step(hh, j, carry, masked):
        m, l, acc = carry
        ks = pl.multiple_of(j * tk, tk)
        s = scores(hh, ks)
        if masked:
            s = s + mask_ref[...]
        m_new = jnp.maximum(m, jnp.max(s, axis=0, keepdims=True))
        p = jnp.exp2(s - m_new)
        alpha = jnp.exp2(m - m_new)
        l = alpha * l + jnp.sum(p, axis=0, keepdims=True)
        vt = vt_ref[0, hh * vr:hh * vr + dh, pl.ds(ks, tk)]
        acc = alpha * acc + _dot(vt, p.astype(BF16))
        return m_new, l, acc

    def weighted_values(hh, ks):
        return _dot(vt_ref[0, hh * vr:(hh + 1) * vr, pl.ds(ks, tk)], p_ref[hh])

    def lazy_block(ks, ks_prev, carry, masked):
        out = []
        for hh in range(heads):
            m, acc, over, shift_prev = carry[hh]
            s = scores(hh, ks)
            acc = (acc + weighted_values(hh, ks_prev)) * jnp.exp2(-shift_prev)
            p = jnp.exp2((s + mask_ref[...] if masked else s) - m).astype(BF16)
            p_ref[hh] = p
            over_b = jnp.log2(jnp.max(p, axis=0, keepdims=True).astype(F32))
            shift = jnp.maximum(over_b, 0.0)
            out.append((m + shift, acc, jnp.maximum(over, over_b), shift))
        return tuple(out)

    def finish(hh, l, acc):
        o_ref[0, hh * dh:(hh + 1) * dh, :] = (acc / l).astype(BF16)

    kd = pl.multiple_of(qi * tk, tk)
    head0 = pl.program_id(1) * heads
    carry = []
    for hh in range(heads):
        zero = jnp.zeros((1, tq), F32)
        carry.append((own_ref[0, pl.ds(head0 + hh, 1), :], jnp.zeros((vr, tq), F32), zero, zero))
    p_ref[...] = jnp.zeros_like(p_ref)

    def blocks(j, n, c):
        for i in range(n):
            prev = jnp.maximum(j + i - 1, 0)
            c = lazy_block(pl.multiple_of((j + i) * tk, tk), pl.multiple_of(prev * tk, tk), c, False)
        return c

    skip0 = (pl.program_id(0) * FOX_HEADS + head0) * nt + qi
    j0 = skip_ref[skip0]
    for hh in range(1, heads):
        j0 = jnp.minimum(j0, skip_ref[skip0 + hh * nt])
    odd = (qi - j0) % 2
    carry = lax.fori_loop(0, odd, lambda j, c: blocks(j0 + j, 1, c), tuple(carry))
    carry = lax.fori_loop(0, (qi - j0) // 2, lambda t, c: blocks(j0 + odd + 2 * t, 2, c), carry)
    carry = lazy_block(kd, pl.multiple_of(jnp.maximum(qi - 1, 0) * tk, tk), carry, True)
    worst = jnp.max(jnp.concatenate([c[2] for c in carry], axis=0))
    for hh in range(heads):
        acc = carry[hh][1] + weighted_values(hh, kd)
        finish(hh, acc[dh:dh + 1], acc[:dh])

    @pl.when(jnp.logical_not(worst <= FOX_LAZY_HEADROOM))
    def _():
        init = (jnp.full((1, tq), -jnp.inf, F32), jnp.zeros((1, tq), F32), jnp.zeros((dh, tq), F32))
        for hh in range(heads):
            c = exact_step(hh, qi, init, True)
            c = lax.fori_loop(0, qi, lambda j, c, hh=hh: exact_step(hh, j, c, False), c)
            finish(hh, c[1], c[2])


def _fox(qaug, kaug, vt, own, skip, batch, seq):
    tq = min(FOX_TQ, seq)
    tk = min(FOX_TK, tq)
    nq = seq // tq
    hp = FOX_HEADS_PER_STEP
    mask = jnp.asarray(np.where(np.arange(tk)[:, None] <= np.arange(tq)[None, :], 0.0, -np.inf), F32)
    nsub = min(TOKEN_TILE, seq) // tk
    table = skip[:, :nsub, :FOX_HEADS].reshape(batch, nq, FOX_HEADS).transpose(0, 2, 1)
    table = table.astype(jnp.int32).reshape(-1)
    tile_map = lambda f: (lambda b, h, i, tbl: f(b, h, i))
    return pl.pallas_call(
        functools.partial(_fox_kernel, tq=tq, tk=tk, heads=hp, nt=nq),
        out_shape=jax.ShapeDtypeStruct((batch, FOX_W, seq), BF16),
        grid_spec=pltpu.PrefetchScalarGridSpec(
            num_scalar_prefetch=1,
            grid=(batch, FOX_HEADS // hp, nq),
            in_specs=[pl.BlockSpec((tq, hp * LANES), tile_map(lambda b, h, i: (b * nq + i, h))),
                      pl.BlockSpec((seq, hp * LANES), tile_map(lambda b, h, i: (b, h))),
                      pl.BlockSpec((1, hp * FOX_VROWS, seq), tile_map(lambda b, h, i: (b, h, 0))),
                      pl.BlockSpec((1, FOX_HEADS, tq), tile_map(lambda b, h, i: (b, 0, i))),
                      pl.BlockSpec((tk, tq), tile_map(lambda b, h, i: (0, 0)),
                                   pipeline_mode=pl.Buffered(1))],
            out_specs=pl.BlockSpec((1, hp * FOX_HEAD_DIM, tq), tile_map(lambda b, h, i: (b, h, i))),
            scratch_shapes=[pltpu.VMEM((hp, tk, tq), BF16)]),
        compiler_params=pltpu.CompilerParams(dimension_semantics=("parallel", "parallel", "arbitrary"),
                                             vmem_limit_bytes=VMEM_LIMIT_BYTES),
        name="fox",
    )(table, qaug, kaug, vt, own, mask)


def _gla_kernel(gq_ref, gk_ref, gv_ref, gr_ref, la_ref, g_ref, tri_ref, o_ref, st_ref, *, nb, tile):
    @pl.when(pl.program_id(0) == 0)
    def _():
        st_ref[...] = jnp.zeros_like(st_ref)

    c = GLA_CHUNK
    lane_head = lax.broadcasted_iota(jnp.int32, (c, GLA_KW), 1) // GLA_HEAD_K
    st_head = lax.broadcasted_iota(jnp.int32, (GLA_HEAD_V, GLA_KW), 1) // GLA_HEAD_K
    causal = (lax.broadcasted_iota(jnp.int32, (c, c), 0) >= lax.broadcasted_iota(jnp.int32, (c, c), 1))
    tri = tri_ref[...]

    def chunk(ci, _):
        rows = pl.ds(pl.multiple_of(ci * c, c), c)
        bh = [(b, h) for b in range(nb) for h in range(GLA_HEADS)]
        vcols = lambda h: slice(h * GLA_HEAD_V, (h + 1) * GLA_HEAD_V)
        bcs = []
        for b in range(nb):
            la = la_ref[b, rows, :]
            la_hi = la.astype(BF16)
            la_lo = (la - la_hi.astype(F32)).astype(BF16)
            bcs.append(_dot(tri, la_hi) + _dot(tri, la_lo))
        qd, kd, ke, st, st_lo, decay = [], [], [], [], [], []
        for b in range(nb):
            bc = bcs[b]
            bl = bc[c - 1:c, :]
            k = gk_ref[b, rows, :]
            qd.append(gq_ref[b, rows, :] * (GLA_HEAD_K ** -0.5) * jnp.exp(bc))
            kd.append((k * jnp.exp(-bc)).astype(BF16))
            ke.append((k * jnp.exp(bl - bc)).astype(BF16))
            decay.append(jnp.exp(bl))
            st.append(st_ref[b])
            st_lo.append(st[b].astype(BF16))
        qdh = {(b, h): jnp.where(lane_head == h, qd[b], 0.0).astype(BF16) for b, h in bh}
        a = {(b, h): jnp.where(causal, _dot_nt(qdh[b, h], kd[b]), 0.0).astype(BF16) for b, h in bh}
        vh = {(b, h): gv_ref[b, rows, vcols(h)] for b, h in bh}
        o = {(b, h): _dot(a[b, h], vh[b, h]) + _dot_nt(qdh[b, h], st_lo[b]) for b, h in bh}
        u = {(b, h): _dot_tn(vh[b, h], ke[b]) for b, h in bh}
        for b, h in bh:
            gr = gr_ref[b, rows, vcols(h)]
            o_ref[b, rows, vcols(h)] = (_rms_scale(o[b, h]) * g_ref[:, vcols(h)]
                                        * (gr * _sigmoid(gr))).astype(BF16)
        for b in range(nb):
            upd = sum(jnp.where(st_head == h, u[b, h], 0.0) for h in range(GLA_HEADS))
            st_ref[b] = st[b] * decay[b] + upd
        return 0

    lax.fori_loop(0, tile // c, chunk, 0)


def _gla(gq, gk, gv, gr, la, g_gla, batch, seq):
    tile = min(GLA_TILE, seq)
    tri = jnp.asarray(np.tril(np.ones((GLA_CHUNK, GLA_CHUNK), np.float32)), BF16)
    blk = lambda w: pl.BlockSpec((batch, tile, w), lambda i: (0, i, 0))
    r3 = lambda x: x.reshape(batch, seq, x.shape[-1])
    out = pl.pallas_call(
        functools.partial(_gla_kernel, nb=batch, tile=tile),
        out_shape=jax.ShapeDtypeStruct((batch, seq, GLA_VW), BF16),
        grid=(seq // tile,),
        in_specs=[blk(GLA_KW), blk(GLA_KW), blk(GLA_VW), blk(GLA_VW), blk(GLA_KW),
                  _const_spec((1, GLA_VW)), _const_spec((GLA_CHUNK, GLA_CHUNK))],
        out_specs=blk(GLA_VW),
        scratch_shapes=[pltpu.VMEM((batch, GLA_HEAD_V, GLA_KW), F32)],
        compiler_params=pltpu.CompilerParams(dimension_semantics=("arbitrary",),
                                             vmem_limit_bytes=VMEM_LIMIT_BYTES),
        name="gla",
    )(r3(gq), r3(gk), r3(gv), r3(gr), r3(la), g_gla, tri)
    return out.reshape(batch * seq, GLA_VW)


def _memkv_kernel(mem_ref, g_ref, w_ref, k_ref, v_ref):
    mn = (_rms_scale(mem_ref[0]) * g_ref[...]).astype(BF16)
    kv = _dot(mn, w_ref[...])
    k_ref[0] = kv[:, :D_MODEL].astype(BF16)
    v_ref[0] = kv[:, D_MODEL:].astype(BF16)


def _memkv(mem, g, w):
    batch = mem.shape[0]
    blk = pl.BlockSpec((1, N_MEM, D_MODEL), lambda b: (b, 0, 0))
    sds = jax.ShapeDtypeStruct((batch, N_MEM, D_MODEL), BF16)
    return pl.pallas_call(
        _memkv_kernel,
        out_shape=(sds, sds),
        grid=(batch,),
        in_specs=[blk, _const_spec((1, D_MODEL)), _const_spec(w.shape)],
        out_specs=(blk, blk),
        compiler_params=pltpu.CompilerParams(dimension_semantics=("parallel",),
                                             vmem_limit_bytes=VMEM_LIMIT_BYTES),
        name="memkv",
    )(mem, g, w)


def _post_kernel(h_ref, oft_ref, og_ref, wof_ref, wog_ref, g1_ref, g2_ref, wq_ref, km_ref, vm_ref,
                 wo_ref, g3_ref, out_ref):
    m = _dot_tn(oft_ref[0], wof_ref[...]) + _dot(og_ref[...], wog_ref[...])
    h2 = h_ref[...] + _rms_scale(m) * g1_ref[...]
    u = (_rms_scale(h2) * g2_ref[...]).astype(BF16)
    q = _dot(u, wq_ref[...]) * (MEM_HEAD_DIM ** -0.5)
    heads = []
    for hd in range(MEM_HEADS):
        cols = slice(hd * MEM_HEAD_DIM, (hd + 1) * MEM_HEAD_DIM)
        s = _dot_nt(q[:, cols].astype(BF16), km_ref[0, :, cols])
        p = jnp.exp(s - jnp.max(s, axis=-1, keepdims=True))
        p = p / jnp.sum(p, axis=-1, keepdims=True)
        heads.append(_dot(p.astype(BF16), vm_ref[0, :, cols]))
    o = jnp.concatenate(heads, axis=1).astype(BF16)
    c = _dot(o, wo_ref[...])
    out_ref[...] = h2 + _rms_scale(c) * g3_ref[...]


def _post(h, oft, og, wof, wog, g1, g2, wq, km, vm, wo, g3, batch, seq):
    tm = min(TOKEN_TILE, seq)
    nt = seq // tm
    row = lambda w: pl.BlockSpec((tm, w), lambda b, i: (b * nt + i, 0))
    memblk = pl.BlockSpec((1, N_MEM, D_MODEL), lambda b, i: (b, 0, 0))
    g = _const_spec((1, D_MODEL))
    return pl.pallas_call(
        _post_kernel,
        out_shape=jax.ShapeDtypeStruct((batch * seq, D_MODEL), F32),
        grid=(batch, nt),
        in_specs=[row(D_MODEL), pl.BlockSpec((1, FOX_W, tm), lambda b, i: (b, 0, i)), row(GLA_VW),
                  _const_spec(wof.shape), _const_spec(wog.shape), g, g, _const_spec(wq.shape),
                  memblk, memblk, _const_spec(wo.shape), g],
        out_specs=row(D_MODEL),
        compiler_params=pltpu.CompilerParams(dimension_semantics=("parallel", "parallel"),
                                             vmem_limit_bytes=VMEM_LIMIT_BYTES),
        name="post",
    )(h, oft, og, wof, wog, g1, g2, wq, km, vm, wo, g3)


def kernel(x, mem, ffn1_g_pre, ffn1_w_gu, ffn1_w_down, ffn1_g_post, mix_g_pre, mix_w_in, mix_w_a2, mix_b_a,
           mix_b_f, mix_g_gla, mix_w_out, mix_g_post, mem_g_pre, mem_g_kv, mem_w_q, mem_w_kv, mem_w_o,
           mem_g_post, ffn2_g_pre, ffn2_w_gu, ffn2_w_down, ffn2_g_post):
    batch, seq, _ = x.shape
    depth = ffn1_g_pre.shape[0]
    row = lambda v: v.reshape(1, -1)
    h = x.reshape(batch * seq, D_MODEL)
    for l in range(depth):
        wgu1, wd1 = _prep_ffn(ffn1_w_gu[l], ffn1_w_down[l])
        wgu2, wd2 = _prep_ffn(ffn2_w_gu[l], ffn2_w_down[l])
        w1, wvt, wa2, bfp, ba = _prep_mix(mix_w_in[l], mix_w_a2[l], mix_b_a[l], mix_b_f[l])
        w_out = mix_w_out[l].astype(BF16)

        h = _ffn(h, row(ffn1_g_pre[l]), wgu1, wd1, row(ffn1_g_post[l]))
        qaug, kaug, vt, gq, gk, gv, gr, la, own, skip = _mix_in(h, row(mix_g_pre[l]), w1, wvt, wa2, bfp, ba,
                                                                batch, seq)
        oft = _fox(qaug, kaug, vt, own, skip, batch, seq)
        og = _gla(gq, gk, gv, gr, la, row(mix_g_gla[l]), batch, seq)
        km, vm = _memkv(mem, row(mem_g_kv[l]), mem_w_kv[l].astype(BF16))
        h = _post(h, oft, og, w_out[:FOX_W], w_out[FOX_W:], row(mix_g_post[l]), row(mem_g_pre[l]),
                  mem_w_q[l].astype(BF16), km, vm, mem_w_o[l].astype(BF16), row(mem_g_post[l]), batch, seq)
        h = _ffn(h, row(ffn2_g_pre[l]), wgu2, wd2, row(ffn2_g_post[l]))
    return h.reshape(batch, seq, D_MODEL)
```

```python
import functools

import numpy as np
import jax
import jax.numpy as jnp
from jax import lax
from jax.experimental import pallas as pl
from jax.experimental.pallas import tpu as pltpu

F32 = jnp.float32
BF16 = jnp.bfloat16

D_MODEL = 1024
FOX_HEADS = 8
FOX_HEAD_DIM = 64
GLA_HEADS = 4
GLA_HEAD_K = 64
GLA_HEAD_V = 128
GLA_GATE_RANK = 16
GLA_GATE_TAU = 16.0
GLA_CHUNK = 64
FOX_W = FOX_HEADS * FOX_HEAD_DIM
GLA_KW = GLA_HEADS * GLA_HEAD_K
GLA_VW = GLA_HEADS * GLA_HEAD_V
IN_SPLITS = (FOX_W, FOX_W, FOX_W, FOX_HEADS, GLA_KW, GLA_KW, GLA_VW, GLA_VW, GLA_GATE_RANK)
N_MEM = 256
MEM_HEADS = 4
MEM_HEAD_DIM = D_MODEL // MEM_HEADS
D_FF = 2816
MACARON_W = 0.5
RMS_EPS = 1e-6
LOG2E = 1.4426950408889634

LANES = 128
VMEM_LIMIT_BYTES = 56 * 1024 * 1024

TOKEN_TILE = 512
FFN_CHUNK = 256
FOX_TQ = 256
FOX_TK = 256
FOX_HEADS_PER_STEP = 4
FOX_LAZY_HEADROOM = 60.0
FOX_SKIP_MARGIN = 160.0
FOX_NORM_SLACK = 1.02
FOX_VROWS = FOX_HEAD_DIM + 16
GLA_TILE = 512
GLA_CHUNKS_PER_STEP = 4

AUG_HI, AUG_MID, AUG_LO, AUG_ONE = 0, 8, 16, 24
GA_LANE = 8


def _rms_scale(x):
    return x * lax.rsqrt(jnp.mean(x * x, axis=-1, keepdims=True) + RMS_EPS)


def _sigmoid(x):
    return 1.0 / (1.0 + jnp.exp(-x))


def _log_sigmoid(x):
    return jnp.minimum(x, 0.0) - jnp.log(1.0 + jnp.exp(-jnp.abs(x)))


def _dot(a, b):
    return jnp.dot(a, b, preferred_element_type=F32)


def _dot_nt(a, b):
    return lax.dot_general(a, b, (((1,), (1,)), ((), ())), preferred_element_type=F32)


def _dot_tn(a, b):
    return lax.dot_general(a, b, (((0,), (0,)), ((), ())), preferred_element_type=F32)


def _split3(x):
    hi = x.astype(BF16)
    r1 = x - hi.astype(F32)
    mid = r1.astype(BF16)
    lo = (r1 - mid.astype(F32)).astype(BF16)
    return hi, mid, lo


def _const_spec(shape):
    zeros = (0,) * len(shape)
    return pl.BlockSpec(shape, lambda *_: zeros, pipeline_mode=pl.Buffered(1))


def _ffn_math(h, g_pre, wgu_ref, wd_ref, g_post):
    xn = (_rms_scale(h) * g_pre).astype(BF16)
    n_chunks = D_FF // FFN_CHUNK
    acc = None
    for j in range(n_chunks):
        gate = _dot(xn, wgu_ref[:, j * FFN_CHUNK:(j + 1) * FFN_CHUNK])
        up = _dot(xn, wgu_ref[:, D_FF + j * FFN_CHUNK:D_FF + (j + 1) * FFN_CHUNK])
        act = (gate * _sigmoid(gate) * up).astype(BF16)
        part = _dot(act, wd_ref[j * FFN_CHUNK:(j + 1) * FFN_CHUNK, :])
        acc = part if acc is None else acc + part
    return h + MACARON_W * (_rms_scale(acc) * g_post)


def _ffn_kernel(h_ref, gpre_ref, wgu_ref, wd_ref, gpost_ref, o_ref):
    o_ref[...] = _ffn_math(h_ref[...], gpre_ref[...], wgu_ref, wd_ref, gpost_ref[...])


def _prep_ffn(w_gu, w_down):
    return w_gu.astype(BF16), w_down.astype(BF16)


def _ffn(h, g_pre, wgu, wd, g_post):
    t = h.shape[0]
    tm = min(TOKEN_TILE, t)
    row = pl.BlockSpec((tm, D_MODEL), lambda i: (i, 0))
    return pl.pallas_call(
        _ffn_kernel,
        out_shape=jax.ShapeDtypeStruct((t, D_MODEL), F32),
        grid=(t // tm,),
        in_specs=[row, _const_spec((1, D_MODEL)), _const_spec(wgu.shape), _const_spec(wd.shape),
                  _const_spec((1, D_MODEL))],
        out_specs=row,
        compiler_params=pltpu.CompilerParams(dimension_semantics=("parallel",),
                                             vmem_limit_bytes=VMEM_LIMIT_BYTES),
        name="ffn",
    )(h, g_pre, wgu, wd, g_post)


MIX_QW = FOX_HEADS * LANES
MIX_G0 = 2 * FOX_W
MIX_GW = 2 * GLA_KW + 2 * GLA_VW + LANES


def _prep_mix(w_in, w_a2, b_a, b_f):
    offs = np.cumsum(IN_SPLITS)[:-1].tolist()
    fq, fk, fv, fz, gq, gk, gv, gr, ga = jnp.split(w_in, offs, axis=1)
    pad = jnp.zeros((D_MODEL, LANES - FOX_HEADS - GLA_GATE_RANK), F32)
    w1 = jnp.concatenate([fq, fk, gq, gk, gv, gr, fz, ga, pad], axis=1).astype(BF16)
    wvt = fv.T.astype(BF16)
    wa2 = jnp.zeros((LANES, GLA_KW), F32).at[GA_LANE:GA_LANE + GLA_GATE_RANK].set(w_a2).astype(BF16)
    bfp = jnp.zeros((1, LANES), F32).at[0, :FOX_HEADS].set(b_f)
    return w1, wvt, wa2, bfp, b_a.reshape(1, GLA_KW)


def _aug_base(h):
    return FOX_HEAD_DIM if h % 2 == 0 else 0


def _aug_placement():
    p = np.zeros((LANES, 2 * MIX_QW), np.float32)
    for h in range(FOX_HEADS):
        qb = h * LANES + _aug_base(h)
        kb = MIX_QW + h * LANES + _aug_base(h)
        for i, src in enumerate((AUG_HI, AUG_MID, AUG_LO)):
            p[src + h, qb + i] = 1.0
            p[AUG_ONE, kb + i] = 1.0
            p[AUG_ONE, qb + 3 + i] = 1.0
            p[src + h, kb + 3 + i] = -1.0
    return jnp.asarray(p, BF16)


def _mix_in_kernel(h_ref, g_ref, w1_ref, wvt_ref, wa2_ref, bf_ref, ba_ref, p_ref, tri_ref, grp_ref, grpt_ref,
                   qaug_ref, kaug_ref, vt_ref, gq_ref, gk_ref, gv_ref, gr_ref, la_ref, own_ref, skip_ref,
                   carry_ref, hist_ref, *, fox_tk):
    @pl.when(pl.program_id(1) == 0)
    def _():
        carry_ref[...] = jnp.zeros_like(carry_ref)
        hist_ref[...] = jnp.full(hist_ref.shape, -jnp.inf, F32)

    tm = h_ref.shape[0]
    xn = (_rms_scale(h_ref[...]) * g_ref[...]).astype(BF16)
    zq = _dot(xn, w1_ref[:, :FOX_W]) * (FOX_HEAD_DIM ** -0.5 * LOG2E)
    zk = _dot(xn, w1_ref[:, FOX_W:MIX_G0])
    zg = _dot(xn, w1_ref[:, MIX_G0:])
    vt = _dot_nt(wvt_ref[...], xn).astype(BF16)
    for hd in range(FOX_HEADS):
        vt_ref[0, hd * FOX_VROWS:hd * FOX_VROWS + FOX_HEAD_DIM, :] = vt[hd * FOX_HEAD_DIM:(hd + 1) * FOX_HEAD_DIM]
        vt_ref[0, hd * FOX_VROWS + FOX_HEAD_DIM:(hd + 1) * FOX_VROWS, :] = jnp.ones(
            (FOX_VROWS - FOX_HEAD_DIM, tm), BF16)

    o = 0
    gq_ref[...] = zg[:, o:o + GLA_KW]; o += GLA_KW
    gk_ref[...] = zg[:, o:o + GLA_KW]; o += GLA_KW
    gv_ref[...] = zg[:, o:o + GLA_VW].astype(BF16); o += GLA_VW
    gr_ref[...] = zg[:, o:o + GLA_VW]; o += GLA_VW
    small = zg[:, o:o + LANES]

    xa = _dot(small.astype(BF16), wa2_ref[...]) + ba_ref[...]
    la_ref[...] = _log_sigmoid(xa) * (1.0 / GLA_GATE_TAU)

    lane = lax.broadcasted_iota(jnp.int32, (tm, LANES), 1)
    logf = jnp.where(lane < FOX_HEADS, _log_sigmoid(small + bf_ref[...]), 0.0)
    tri = tri_ref[...]
    l_hi, l_mid, l_lo = _split3(logf)
    l_all = (l_hi.astype(F32) + pltpu.roll(l_mid.astype(F32), AUG_MID, 1)
             + pltpu.roll(l_lo.astype(F32), AUG_LO, 1)).astype(BF16)
    part = _dot(tri, l_all)
    cum = part + pltpu.roll(part, LANES - AUG_MID, 1) + pltpu.roll(part, LANES - AUG_LO, 1)
    cum = jnp.where(lane < FOX_HEADS, cum, 0.0) + carry_ref[0:1, :]
    carry_ref[0:1, :] = cum[tm - 1:tm, :]

    c2 = cum * LOG2E
    own_ref[0] = _dot_nt(grpt_ref[...], (zq * zk).astype(BF16))[:FOX_HEADS]

    qn2 = _dot((zq * zq).astype(BF16), grp_ref[...])
    kn2 = _dot((zk * zk).astype(BF16), grp_ref[...])
    nsub = tm // fox_tk
    rows = []
    kn_run = carry_ref[1:2, :]
    for sb in range(nsub):
        r0 = sb * fox_tk
        kn_run = jnp.maximum(kn_run, jnp.sqrt(jnp.max(kn2[r0:r0 + fox_tk], axis=0, keepdims=True)))
        qn = jnp.sqrt(jnp.max(qn2[r0:r0 + fox_tk], axis=0, keepdims=True))
        reach = 2.0 * (FOX_NORM_SLACK * FOX_NORM_SLACK) * qn * kn_run + c2[r0:r0 + 1] + FOX_SKIP_MARGIN
        rows.append(jnp.sum(jnp.where(reach < hist_ref[...], 1.0, 0.0), axis=0, keepdims=True))
        hist_ref[pl.ds(pl.program_id(1) * nsub + sb, 1), :] = c2[r0 + fox_tk - 1:r0 + fox_tk]
    carry_ref[1:2, :] = kn_run
    skip_ref[0] = jnp.concatenate(rows + [jnp.zeros((8 - nsub, LANES), F32)], axis=0)

    c_hi, c_mid, c_lo = _split3(c2)
    packed = (c_hi.astype(F32) + pltpu.roll(c_mid.astype(F32), AUG_MID, 1)
              + pltpu.roll(c_lo.astype(F32), AUG_LO, 1))
    packed = jnp.where(lane == AUG_ONE, 1.0, packed)
    aug = _dot(packed.astype(BF16), p_ref[...])
    low_half = lane < FOX_HEAD_DIM
    for hd in range(FOX_HEADS):
        pair = slice((hd // 2) * LANES, (hd // 2 + 1) * LANES)
        cols = slice(hd * LANES, (hd + 1) * LANES)
        own = low_half if hd % 2 == 0 else jnp.logical_not(low_half)
        qaug_ref[:, cols] = (jnp.where(own, zq[:, pair], 0.0) + aug[:, cols]).astype(BF16)
        kaug_ref[:, cols] = (jnp.where(own, zk[:, pair], 0.0)
                             + aug[:, MIX_QW + hd * LANES:MIX_QW + (hd + 1) * LANES]).astype(BF16)


def _mix_in(h, g, w1, wvt, wa2, bfp, ba, batch, seq):
    tm = min(TOKEN_TILE, seq)
    nt = seq // tm
    tri = jnp.asarray(np.tril(np.ones((tm, tm), np.float32)), BF16)
    grp_np = np.arange(FOX_W)[:, None] // FOX_HEAD_DIM == np.arange(LANES)[None, :]
    grp = jnp.asarray(grp_np, BF16)
    grpt = jnp.asarray(grp_np[:, :16].T, BF16)
    fox_tk = min(FOX_TK, seq)
    row = lambda w: pl.BlockSpec((tm, w), lambda b, i: (b * nt + i, 0))
    t = batch * seq
    sds = jax.ShapeDtypeStruct
    return pl.pallas_call(
        functools.partial(_mix_in_kernel, fox_tk=fox_tk),
        out_shape=(sds((t, MIX_QW), BF16), sds((t, MIX_QW), BF16),
                   sds((batch, FOX_HEADS * FOX_VROWS, seq), BF16),
                   sds((t, GLA_KW), F32), sds((t, GLA_KW), F32), sds((t, GLA_VW), BF16),
                   sds((t, GLA_VW), F32), sds((t, GLA_KW), F32), sds((batch, FOX_HEADS, seq), F32),
                   sds((batch * nt, 8, LANES), F32)),
        grid=(batch, nt),
        in_specs=[row(D_MODEL), _const_spec((1, D_MODEL)), _const_spec(w1.shape), _const_spec(wvt.shape),
                  _const_spec(wa2.shape), _const_spec(bfp.shape), _const_spec(ba.shape),
                  _const_spec((LANES, 2 * MIX_QW)), _const_spec((tm, tm)), _const_spec((FOX_W, LANES)),
                  _const_spec((16, FOX_W))],
        out_specs=(row(MIX_QW), row(MIX_QW),
                   pl.BlockSpec((1, FOX_HEADS * FOX_VROWS, tm), lambda b, i: (b, 0, i)),
                   row(GLA_KW), row(GLA_KW), row(GLA_VW), row(GLA_VW), row(GLA_KW),
                   pl.BlockSpec((1, FOX_HEADS, tm), lambda b, i: (b, 0, i)),
                   pl.BlockSpec((1, 8, LANES), lambda b, i: (b * nt + i, 0, 0))),
        scratch_shapes=[pltpu.VMEM((2, LANES), F32), pltpu.VMEM((seq // fox_tk, LANES), F32)],
        compiler_params=pltpu.CompilerParams(dimension_semantics=("arbitrary", "arbitrary"),
                                             vmem_limit_bytes=VMEM_LIMIT_BYTES),
        name="mix_in",
    )(h, g, w1, wvt, wa2, bfp, ba, _aug_placement(), tri, grp, grpt)


def _fox_kernel(skip_ref, q_ref, k_ref, vt_ref, own_ref, mask_ref, o_ref, p_ref, *, tq, tk, heads, nt):
    qi = pl.program_id(2)
    dh = FOX_HEAD_DIM
    vr = FOX_VROWS
    assert tq == tk

    def scores(hh, ks):
        cols = slice(hh * LANES, (hh + 1) * LANES)
        return _dot_nt(k_ref[pl.ds(ks, tk), cols], q_ref[:, cols])

    def exact_step(hh, j, carry, masked):
        m, l, acc = carry
        ks = pl.multiple_of(j * tk, tk)
        s = scores(hh, ks)
        if masked:
            s = s + mask_ref[...]
        m_new = jnp.maximum(m, jnp.max(s, axis=0, keepdims=True))
        p = jnp.exp2(s - m_new)
        alpha = jnp.exp2(m - m_new)
        l = alpha * l + jnp.sum(p, axis=0, keepdims=True)
        vt = vt_ref[0, hh * vr:hh * vr + dh, pl.ds(ks, tk)]
        acc = alpha * acc + _dot(vt, p.astype(BF16))
        return m_new, l, acc

    def weighted_values(hh, ks):
        return _dot(vt_ref[0, hh * vr:(hh + 1) * vr, pl.ds(ks, tk)], p_ref[hh])

    def lazy_block(ks, ks_prev, carry, masked):
        out = []
        for hh in range(heads):
            m, acc, over, shift_prev = carry[hh]
            s = scores(hh, ks)
            acc = (acc + weighted_values(hh, ks_prev)) * jnp.exp2(-shift_prev)
            p = jnp.exp2((s + mask_ref[...] if masked else s) - m).astype(BF16)
            p_ref[hh] = p
            over_b = jnp.log2(jnp.max(p, axis=0, keepdims=True).astype(F32))
            shift = jnp.maximum(over_b, 0.0)
            out.append((m + shift, acc, jnp.maximum(over, over_b), shift))
        return tuple(out)

    def finish(hh, l, acc):
        o_ref[0, hh * dh:(hh + 1) * dh, :] = (acc / l).astype(BF16)

    kd = pl.multiple_of(qi * tk, tk)
    head0 = pl.program_id(1) * heads
    carry = []
    for hh in range(heads):
        zero = jnp.zeros((1, tq), F32)
        carry.append((own_ref[0, pl.ds(head0 + hh, 1), :], jnp.zeros((vr, tq), F32), zero, zero))
    p_ref[...] = jnp.zeros_like(p_ref)

    def blocks(j, n, c):
        for i in range(n):
            prev = jnp.maximum(j + i - 1, 0)
            c = lazy_block(pl.multiple_of((j + i) * tk, tk), pl.multiple_of(prev * tk, tk), c, False)
        return c

    skip0 = (pl.program_id(0) * FOX_HEADS + head0) * nt + qi
    j0 = skip_ref[skip0]
    for hh in range(1, heads):
        j0 = jnp.minimum(j0, skip_ref[skip0 + hh * nt])
    odd = (qi - j0) % 2
    carry = lax.fori_loop(0, odd, lambda j, c: blocks(j0 + j, 1, c), tuple(carry))
    carry = lax.fori_loop(0, (qi - j0) // 2, lambda t, c: blocks(j0 + odd + 2 * t, 2, c), carry)
    carry = lazy_block(kd, pl.multiple_of(jnp.maximum(qi - 1, 0) * tk, tk), carry, True)
    worst = jnp.max(jnp.concatenate([c[2] for c in carry], axis=0))
    for hh in range(heads):
        acc = carry[hh][1] + weighted_values(hh, kd)
        finish(hh, acc[dh:dh + 1], acc[:dh])

    @pl.when(jnp.logical_not(worst <= FOX_LAZY_HEADROOM))
    def _():
        init = (jnp.full((1, tq), -jnp.inf, F32), jnp.zeros((1, tq), F32), jnp.zeros((dh, tq), F32))
        for hh in range(heads):
            c = exact_step(hh, qi, init, True)
            c = lax.fori_loop(0, qi, lambda j, c, hh=hh: exact_step(hh, j, c, False), c)
            finish(hh, c[1], c[2])


def _fox(qaug, kaug, vt, own, skip, batch, seq):
    tq = min(FOX_TQ, seq)
    tk = min(FOX_TK, tq)
    nq = seq // tq
    hp = FOX_HEADS_PER_STEP
    mask = jnp.asarray(np.where(np.arange(tk)[:, None] <= np.arange(tq)[None, :], 0.0, -np.inf), F32)
    nsub = min(TOKEN_TILE, seq) // tk
    table = skip[:, :nsub, :FOX_HEADS].reshape(batch, nq, FOX_HEADS).transpose(0, 2, 1)
    table = table.astype(jnp.int32).reshape(-1)
    tile_map = lambda f: (lambda b, h, i, tbl: f(b, h, i))
    return pl.pallas_call(
        functools.partial(_fox_kernel, tq=tq, tk=tk, heads=hp, nt=nq),
        out_shape=jax.ShapeDtypeStruct((batch, FOX_W, seq), BF16),
        grid_spec=pltpu.PrefetchScalarGridSpec(
            num_scalar_prefetch=1,
            grid=(batch, FOX_HEADS // hp, nq),
            in_specs=[pl.BlockSpec((tq, hp * LANES), tile_map(lambda b, h, i: (b * nq + i, h))),
                      pl.BlockSpec((seq, hp * LANES), tile_map(lambda b, h, i: (b, h))),
                      pl.BlockSpec((1, hp * FOX_VROWS, seq), tile_map(lambda b, h, i: (b, h, 0))),
                      pl.BlockSpec((1, FOX_HEADS, tq), tile_map(lambda b, h, i: (b, 0, i))),
                      pl.BlockSpec((tk, tq), tile_map(lambda b, h, i: (0, 0)),
                                   pipeline_mode=pl.Buffered(1))],
            out_specs=pl.BlockSpec((1, hp * FOX_HEAD_DIM, tq), tile_map(lambda b, h, i: (b, h, i))),
            scratch_shapes=[pltpu.VMEM((hp, tk, tq), BF16)]),
        compiler_params=pltpu.CompilerParams(dimension_semantics=("parallel", "parallel", "arbitrary"),
                                             vmem_limit_bytes=VMEM_LIMIT_BYTES),
        name="fox",
    )(table, qaug, kaug, vt, own, mask)


def _gla_kernel(gq_ref, gk_ref, gv_ref, gr_ref, la_ref, g_ref, tri_ref, o_ref, st_ref, *, nb, tile):
    @pl.when(pl.program_id(0) == 0)
    def _():
        st_ref[...] = jnp.zeros_like(st_ref)

    c = GLA_CHUNK
    lane_head = lax.broadcasted_iota(jnp.int32, (c, GLA_KW), 1) // GLA_HEAD_K
    st_head = lax.broadcasted_iota(jnp.int32, (GLA_HEAD_V, GLA_KW), 1) // GLA_HEAD_K
    causal = (lax.broadcasted_iota(jnp.int32, (c, c), 0) >= lax.broadcasted_iota(jnp.int32, (c, c), 1))
    tri = tri_ref[...]

    def chunks(ci, _):
        n = GLA_CHUNKS_PER_STEP
        rows = [pl.ds(pl.multiple_of((ci * n + i) * c, c), c) for i in range(n)]
        ib = [(i, b) for i in range(n) for b in range(nb)]
        ibh = [(i, b, h) for i, b in ib for h in range(GLA_HEADS)]
        vcols = lambda h: slice(h * GLA_HEAD_V, (h + 1) * GLA_HEAD_V)
        bc = {}
        for i, b in ib:
            la = la_ref[b, rows[i], :]
            la_hi = la.astype(BF16)
            la_lo = (la - la_hi.astype(F32)).astype(BF16)
            bc[i, b] = _dot(tri, la_hi) + _dot(tri, la_lo)
        qd, kd, ke, decay = {}, {}, {}, {}
        for i, b in ib:
            bl = bc[i, b][c - 1:c, :]
            k = gk_ref[b, rows[i], :]
            qd[i, b] = gq_ref[b, rows[i], :] * (GLA_HEAD_K ** -0.5) * jnp.exp(bc[i, b])
            kd[i, b] = (k * jnp.exp(-bc[i, b])).astype(BF16)
            ke[i, b] = (k * jnp.exp(bl - bc[i, b])).astype(BF16)
            decay[i, b] = jnp.exp(bl)
        qdh = {(i, b, h): jnp.where(lane_head == h, qd[i, b], 0.0).astype(BF16) for i, b, h in ibh}
        a = {(i, b, h): jnp.where(causal, _dot_nt(qdh[i, b, h], kd[i, b]), 0.0).astype(BF16)
             for i, b, h in ibh}
        vh = {(i, b, h): gv_ref[b, rows[i], vcols(h)] for i, b, h in ibh}
        o_intra = {(i, b, h): _dot(a[i, b, h], vh[i, b, h]) for i, b, h in ibh}
        u = {(i, b, h): _dot_tn(vh[i, b, h], ke[i, b]) for i, b, h in ibh}
        st = [st_ref[b] for b in range(nb)]
        for i in range(n):
            st_lo = [s.astype(BF16) for s in st]
            o = {(b, h): o_intra[i, b, h] + _dot_nt(qdh[i, b, h], st_lo[b])
                 for b in range(nb) for h in range(GLA_HEADS)}
            for (b, h), val in o.items():
                gr = gr_ref[b, rows[i], vcols(h)]
                o_ref[b, rows[i], vcols(h)] = (_rms_scale(val) * g_ref[:, vcols(h)]
                                               * (gr * _sigmoid(gr))).astype(BF16)
            st = [st[b] * decay[i, b] + sum(jnp.where(st_head == h, u[i, b, h], 0.0)
                                             for h in range(GLA_HEADS)) for b in range(nb)]
        for b in range(nb):
            st_ref[b] = st[b]
        return 0

    lax.fori_loop(0, tile // (c * GLA_CHUNKS_PER_STEP), chunks, 0)


def _gla(gq, gk, gv, gr, la, g_gla, batch, seq):
    tile = min(GLA_TILE, seq)
    tri = jnp.asarray(np.tril(np.ones((GLA_CHUNK, GLA_CHUNK), np.float32)), BF16)
    blk = lambda w: pl.BlockSpec((batch, tile, w), lambda i: (0, i, 0))
    r3 = lambda x: x.reshape(batch, seq, x.shape[-1])
    out = pl.pallas_call(
        functools.partial(_gla_kernel, nb=batch, tile=tile),
        out_shape=jax.ShapeDtypeStruct((batch, seq, GLA_VW), BF16),
        grid=(seq // tile,),
        in_specs=[blk(GLA_KW), blk(GLA_KW), blk(GLA_VW), blk(GLA_VW), blk(GLA_KW),
                  _const_spec((1, GLA_VW)), _const_spec((GLA_CHUNK, GLA_CHUNK))],
        out_specs=blk(GLA_VW),
        scratch_shapes=[pltpu.VMEM((batch, GLA_HEAD_V, GLA_KW), F32)],
        compiler_params=pltpu.CompilerParams(dimension_semantics=("arbitrary",),
                                             vmem_limit_bytes=VMEM_LIMIT_BYTES),
        name="gla",
    )(r3(gq), r3(gk), r3(gv), r3(gr), r3(la), g_gla, tri)
    return out.reshape(batch * seq, GLA_VW)


def _memkv_kernel(mem_ref, g_ref, w_ref, k_ref, v_ref):
    mn = (_rms_scale(mem_ref[0]) * g_ref[...]).astype(BF16)
    kv = _dot(mn, w_ref[...])
    k_ref[0] = kv[:, :D_MODEL].astype(BF16)
    v_ref[0] = kv[:, D_MODEL:].astype(BF16)


def _memkv(mem, g, w):
    batch = mem.shape[0]
    blk = pl.BlockSpec((1, N_MEM, D_MODEL), lambda b: (b, 0, 0))
    sds = jax.ShapeDtypeStruct((batch, N_MEM, D_MODEL), BF16)
    return pl.pallas_call(
        _memkv_kernel,
        out_shape=(sds, sds),
        grid=(batch,),
        in_specs=[blk, _const_spec((1, D_MODEL)), _const_spec(w.shape)],
        out_specs=(blk, blk),
        compiler_params=pltpu.CompilerParams(dimension_semantics=("parallel",),
                                             vmem_limit_bytes=VMEM_LIMIT_BYTES),
        name="memkv",
    )(mem, g, w)


def _post_kernel(h_ref, oft_ref, og_ref, wof_ref, wog_ref, g1_ref, g2_ref, wq_ref, km_ref, vm_ref,
                 wo_ref, g3_ref, out_ref):
    m = _dot_tn(oft_ref[0], wof_ref[...]) + _dot(og_ref[...], wog_ref[...])
    h2 = h_ref[...] + _rms_scale(m) * g1_ref[...]
    u = (_rms_scale(h2) * g2_ref[...]).astype(BF16)
    q = _dot(u, wq_ref[...]) * (MEM_HEAD_DIM ** -0.5)
    cols = [slice(hd * MEM_HEAD_DIM, (hd + 1) * MEM_HEAD_DIM) for hd in range(MEM_HEADS)]
    s = [_dot_nt(q[:, c].astype(BF16), km_ref[0, :, c]) for c in cols]
    p = [jnp.exp(x - jnp.max(x, axis=-1, keepdims=True)) for x in s]
    p = [(x / jnp.sum(x, axis=-1, keepdims=True)).astype(BF16) for x in p]
    o = jnp.concatenate([_dot(x, vm_ref[0, :, c]) for x, c in zip(p, cols)], axis=1).astype(BF16)
    c = _dot(o, wo_ref[...])
    out_ref[...] = h2 + _rms_scale(c) * g3_ref[...]


def _post(h, oft, og, wof, wog, g1, g2, wq, km, vm, wo, g3, batch, seq):
    tm = min(TOKEN_TILE, seq)
    nt = seq // tm
    row = lambda w: pl.BlockSpec((tm, w), lambda b, i: (b * nt + i, 0))
    memblk = pl.BlockSpec((1, N_MEM, D_MODEL), lambda b, i: (b, 0, 0))
    g = _const_spec((1, D_MODEL))
    return pl.pallas_call(
        _post_kernel,
        out_shape=jax.ShapeDtypeStruct((batch * seq, D_MODEL), F32),
        grid=(batch, nt),
        in_specs=[row(D_MODEL), pl.BlockSpec((1, FOX_W, tm), lambda b, i: (b, 0, i)), row(GLA_VW),
                  _const_spec(wof.shape), _const_spec(wog.shape), g, g, _const_spec(wq.shape),
                  memblk, memblk, _const_spec(wo.shape), g],
        out_specs=row(D_MODEL),
        compiler_params=pltpu.CompilerParams(dimension_semantics=("parallel", "parallel"),
                                             vmem_limit_bytes=VMEM_LIMIT_BYTES),
        name="post",
    )(h, oft, og, wof, wog, g1, g2, wq, km, vm, wo, g3)


def kernel(x, mem, ffn1_g_pre, ffn1_w_gu, ffn1_w_down, ffn1_g_post, mix_g_pre, mix_w_in, mix_w_a2, mix_b_a,
           mix_b_f, mix_g_gla, mix_w_out, mix_g_post, mem_g_pre, mem_g_kv, mem_w_q, mem_w_kv, mem_w_o,
           mem_g_post, ffn2_g_pre, ffn2_w_gu, ffn2_w_down, ffn2_g_post):
    batch, seq, _ = x.shape
    depth = ffn1_g_pre.shape[0]
    row = lambda v: v.reshape(1, -1)
    h = x.reshape(batch * seq, D_MODEL)
    for l in range(depth):
        wgu1, wd1 = _prep_ffn(ffn1_w_gu[l], ffn1_w_down[l])
        wgu2, wd2 = _prep_ffn(ffn2_w_gu[l], ffn2_w_down[l])
        w1, wvt, wa2, bfp, ba = _prep_mix(mix_w_in[l], mix_w_a2[l], mix_b_a[l], mix_b_f[l])
        w_out = mix_w_out[l].astype(BF16)

        h = _ffn(h, row(ffn1_g_pre[l]), wgu1, wd1, row(ffn1_g_post[l]))
        qaug, kaug, vt, gq, gk, gv, gr, la, own, skip = _mix_in(h, row(mix_g_pre[l]), w1, wvt, wa2, bfp, ba,
                                                                batch, seq)
        oft = _fox(qaug, kaug, vt, own, skip, batch, seq)
        og = _gla(gq, gk, gv, gr, la, row(mix_g_gla[l]), batch, seq)
        km, vm = _memkv(mem, row(mem_g_kv[l]), mem_w_kv[l].astype(BF16))
        h = _post(h, oft, og, w_out[:FOX_W], w_out[FOX_W:], row(mix_g_post[l]), row(mem_g_pre[l]),
                  mem_w_q[l].astype(BF16), km, vm, mem_w_o[l].astype(BF16), row(mem_g_post[l]), batch, seq)
        h = _ffn(h, row(ffn2_g_pre[l]), wgu2, wd2, row(ffn2_g_post[l]))
    return h.reshape(batch, seq, D_MODEL)
```

```python
import functools

import numpy as np
import jax
import jax.numpy as jnp
from jax import lax
from jax.experimental import pallas as pl
from jax.experimental.pallas import tpu as pltpu

F32 = jnp.float32
BF16 = jnp.bfloat16

D_MODEL = 1024
FOX_HEADS = 8
FOX_HEAD_DIM = 64
GLA_HEADS = 4
GLA_HEAD_K = 64
GLA_HEAD_V = 128
GLA_GATE_RANK = 16
GLA_GATE_TAU = 16.0
GLA_CHUNK = 64
FOX_W = FOX_HEADS * FOX_HEAD_DIM
GLA_KW = GLA_HEADS * GLA_HEAD_K
GLA_VW = GLA_HEADS * GLA_HEAD_V
IN_SPLITS = (FOX_W, FOX_W, FOX_W, FOX_HEADS, GLA_KW, GLA_KW, GLA_VW, GLA_VW, GLA_GATE_RANK)
N_MEM = 256
MEM_HEADS = 4
MEM_HEAD_DIM = D_MODEL // MEM_HEADS
D_FF = 2816
MACARON_W = 0.5
RMS_EPS = 1e-6
LOG2E = 1.4426950408889634

LANES = 128
VMEM_LIMIT_BYTES = 56 * 1024 * 1024

TOKEN_TILE = 512
FFN_TILE = 1024
FFN_SPLIT = 2
FFN_SIDE_ROWS = 64
FFN_CHUNK = 256
FOX_TQ = 256
FOX_TK = 256
FOX_HEADS_PER_STEP = 4
FOX_LAZY_HEADROOM = 60.0
FOX_SKIP_MARGIN = 160.0
FOX_NORM_SLACK = 1.02
FOX_VROWS = FOX_HEAD_DIM + 16
GLA_TILE = 512
GLA_CHUNKS_PER_STEP = 4

AUG_HI, AUG_MID, AUG_LO, AUG_ONE = 0, 8, 16, 24
GA_LANE = 8


def _rms_scale(x):
    return x * lax.rsqrt(jnp.mean(x * x, axis=-1, keepdims=True) + RMS_EPS)


def _sigmoid(x):
    return 1.0 / (1.0 + jnp.exp(-x))


def _log_sigmoid(x):
    return jnp.minimum(x, 0.0) - jnp.log(1.0 + jnp.exp(-jnp.abs(x)))


def _dot(a, b):
    return jnp.dot(a, b, preferred_element_type=F32)


def _dot_nt(a, b):
    return lax.dot_general(a, b, (((1,), (1,)), ((), ())), preferred_element_type=F32)


def _dot_tn(a, b):
    return lax.dot_general(a, b, (((0,), (0,)), ((), ())), preferred_element_type=F32)


def _split3(x):
    hi = x.astype(BF16)
    r1 = x - hi.astype(F32)
    mid = r1.astype(BF16)
    lo = (r1 - mid.astype(F32)).astype(BF16)
    return hi, mid, lo


def _const_spec(shape):
    zeros = (0,) * len(shape)
    return pl.BlockSpec(shape, lambda *_: zeros, pipeline_mode=pl.Buffered(1))


def _ffn_kernel(h_ref, gpre_ref, wgu_ref, wd_ref, gpost_ref, o_ref):
    n_chunks = D_FF // FFN_CHUNK
    rows = h_ref.shape[0] // FFN_SPLIT
    n_slices = min(n_chunks, rows // FFN_SIDE_ROWS)
    slices = [slice(s * (rows // n_slices), (s + 1) * (rows // n_slices)) for s in range(n_slices)]

    def shifted(sl, r):
        return slice(r * rows + sl.start, r * rows + sl.stop)

    def opening(sl):
        return (_rms_scale(h_ref[sl, :]) * gpre_ref[...]).astype(BF16)

    def closing(sl, acc_rows):
        o_ref[sl, :] = h_ref[sl, :] + MACARON_W * (_rms_scale(acc_rows) * gpost_ref[...])

    xn = opening(slice(0, rows))
    prev_acc = None
    for r in range(FFN_SPLIT):
        acc, nxt = None, []
        for j in range(n_chunks):
            gate = _dot(xn, wgu_ref[:, j * FFN_CHUNK:(j + 1) * FFN_CHUNK])
            up = _dot(xn, wgu_ref[:, D_FF + j * FFN_CHUNK:D_FF + (j + 1) * FFN_CHUNK])
            act = (gate * _sigmoid(gate) * up).astype(BF16)
            part = _dot(act, wd_ref[j * FFN_CHUNK:(j + 1) * FFN_CHUNK, :])
            acc = part if acc is None else acc + part
            if j < n_slices:
                if r + 1 < FFN_SPLIT:
                    nxt.append(opening(shifted(slices[j], r + 1)))
                if prev_acc is not None:
                    closing(shifted(slices[j], r - 1), prev_acc[slices[j], :])
        prev_acc = acc
        if nxt:
            xn = jnp.concatenate(nxt, axis=0)
    closing(slice((FFN_SPLIT - 1) * rows, FFN_SPLIT * rows), prev_acc)


def _prep_ffn(w_gu, w_down):
    return w_gu.astype(BF16), w_down.astype(BF16)


def _ffn(h, g_pre, wgu, wd, g_post):
    t = h.shape[0]
    tm = min(FFN_TILE, t)
    assert t % tm == 0
    row = pl.BlockSpec((tm, D_MODEL), lambda i: (i, 0))
    return pl.pallas_call(
        _ffn_kernel,
        out_shape=jax.ShapeDtypeStruct((t, D_MODEL), F32),
        grid=(t // tm,),
        in_specs=[row, _const_spec((1, D_MODEL)), _const_spec(wgu.shape), _const_spec(wd.shape),
                  _const_spec((1, D_MODEL))],
        out_specs=row,
        compiler_params=pltpu.CompilerParams(dimension_semantics=("parallel",),
                                             vmem_limit_bytes=VMEM_LIMIT_BYTES),
        name="ffn",
    )(h, g_pre, wgu, wd, g_post)


MIX_QW = FOX_HEADS * LANES
MIX_G0 = 2 * FOX_W
MIX_GW = 2 * GLA_KW + 2 * GLA_VW + LANES


def _prep_mix(w_in, w_a2, b_a, b_f):
    offs = np.cumsum(IN_SPLITS)[:-1].tolist()
    fq, fk, fv, fz, gq, gk, gv, gr, ga = jnp.split(w_in, offs, axis=1)
    pad = jnp.zeros((D_MODEL, LANES - FOX_HEADS - GLA_GATE_RANK), F32)
    w1 = jnp.concatenate([fq, fk, gq, gk, gv, gr, fz, ga, pad], axis=1).astype(BF16)
    wvt = fv.T.astype(BF16)
    wa2 = jnp.zeros((LANES, GLA_KW), F32).at[GA_LANE:GA_LANE + GLA_GATE_RANK].set(w_a2).astype(BF16)
    bfp = jnp.zeros((1, LANES), F32).at[0, :FOX_HEADS].set(b_f)
    return w1, wvt, wa2, bfp, b_a.reshape(1, GLA_KW)


def _aug_base(h):
    return FOX_HEAD_DIM if h % 2 == 0 else 0


def _aug_placement():
    p = np.zeros((LANES, 2 * MIX_QW), np.float32)
    for h in range(FOX_HEADS):
        qb = h * LANES + _aug_base(h)
        kb = MIX_QW + h * LANES + _aug_base(h)
        for i, src in enumerate((AUG_HI, AUG_MID, AUG_LO)):
            p[src + h, qb + i] = 1.0
            p[AUG_ONE, kb + i] = 1.0
            p[AUG_ONE, qb + 3 + i] = 1.0
            p[src + h, kb + 3 + i] = -1.0
    return jnp.asarray(p, BF16)


def _mix_in_kernel(h_ref, g_ref, w1_ref, wvt_ref, wa2_ref, bf_ref, ba_ref, p_ref, tri_ref, grp_ref, grpt_ref,
                   qaug_ref, kaug_ref, vt_ref, gq_ref, gk_ref, gv_ref, gr_ref, la_ref, own_ref, skip_ref,
                   carry_ref, hist_ref, *, fox_tk):
    @pl.when(pl.program_id(1) == 0)
    def _():
        carry_ref[...] = jnp.zeros_like(carry_ref)
        hist_ref[...] = jnp.full(hist_ref.shape, -jnp.inf, F32)

    tm = h_ref.shape[0]
    xn = (_rms_scale(h_ref[...]) * g_ref[...]).astype(BF16)
    zq = _dot(xn, w1_ref[:, :FOX_W]) * (FOX_HEAD_DIM ** -0.5 * LOG2E)
    zk = _dot(xn, w1_ref[:, FOX_W:MIX_G0])
    zg = _dot(xn, w1_ref[:, MIX_G0:])
    vt = _dot_nt(wvt_ref[...], xn).astype(BF16)
    for hd in range(FOX_HEADS):
        vt_ref[0, hd * FOX_VROWS:hd * FOX_VROWS + FOX_HEAD_DIM, :] = vt[hd * FOX_HEAD_DIM:(hd + 1) * FOX_HEAD_DIM]
        vt_ref[0, hd * FOX_VROWS + FOX_HEAD_DIM:(hd + 1) * FOX_VROWS, :] = jnp.ones(
            (FOX_VROWS - FOX_HEAD_DIM, tm), BF16)

    o = 0
    gq_ref[...] = zg[:, o:o + GLA_KW]; o += GLA_KW
    gk_ref[...] = zg[:, o:o + GLA_KW]; o += GLA_KW
    gv_ref[...] = zg[:, o:o + GLA_VW].astype(BF16); o += GLA_VW
    gr_ref[...] = zg[:, o:o + GLA_VW]; o += GLA_VW
    small = zg[:, o:o + LANES]

    xa = _dot(small.astype(BF16), wa2_ref[...]) + ba_ref[...]
    la_ref[...] = _log_sigmoid(xa) * (1.0 / GLA_GATE_TAU)

    lane = lax.broadcasted_iota(jnp.int32, (tm, LANES), 1)
    logf = jnp.where(lane < FOX_HEADS, _log_sigmoid(small + bf_ref[...]), 0.0)
    tri = tri_ref[...]
    l_hi, l_mid, l_lo = _split3(logf)
    l_all = (l_hi.astype(F32) + pltpu.roll(l_mid.astype(F32), AUG_MID, 1)
             + pltpu.roll(l_lo.astype(F32), AUG_LO, 1)).astype(BF16)
    part = _dot(tri, l_all)
    cum = part + pltpu.roll(part, LANES - AUG_MID, 1) + pltpu.roll(part, LANES - AUG_LO, 1)
    cum = jnp.where(lane < FOX_HEADS, cum, 0.0) + carry_ref[0:1, :]
    carry_ref[0:1, :] = cum[tm - 1:tm, :]

    c2 = cum * LOG2E
    own_ref[0] = _dot_nt(grpt_ref[...], (zq * zk).astype(BF16))[:FOX_HEADS]

    qn2 = _dot((zq * zq).astype(BF16), grp_ref[...])
    kn2 = _dot((zk * zk).astype(BF16), grp_ref[...])
    nsub = tm // fox_tk
    rows = []
    kn_run = carry_ref[1:2, :]
    for sb in range(nsub):
        r0 = sb * fox_tk
        kn_run = jnp.maximum(kn_run, jnp.sqrt(jnp.max(kn2[r0:r0 + fox_tk], axis=0, keepdims=True)))
        qn = jnp.sqrt(jnp.max(qn2[r0:r0 + fox_tk], axis=0, keepdims=True))
        reach = 2.0 * (FOX_NORM_SLACK * FOX_NORM_SLACK) * qn * kn_run + c2[r0:r0 + 1] + FOX_SKIP_MARGIN
        rows.append(jnp.sum(jnp.where(reach < hist_ref[...], 1.0, 0.0), axis=0, keepdims=True))
        hist_ref[pl.ds(pl.program_id(1) * nsub + sb, 1), :] = c2[r0 + fox_tk - 1:r0 + fox_tk]
    carry_ref[1:2, :] = kn_run
    skip_ref[0] = jnp.concatenate(rows + [jnp.zeros((8 - nsub, LANES), F32)], axis=0)

    c_hi, c_mid, c_lo = _split3(c2)
    packed = (c_hi.astype(F32) + pltpu.roll(c_mid.astype(F32), AUG_MID, 1)
              + pltpu.roll(c_lo.astype(F32), AUG_LO, 1))
    packed = jnp.where(lane == AUG_ONE, 1.0, packed)
    aug = _dot(packed.astype(BF16), p_ref[...])
    low_half = lane < FOX_HEAD_DIM
    for hd in range(FOX_HEADS):
        pair = slice((hd // 2) * LANES, (hd // 2 + 1) * LANES)
        cols = slice(hd * LANES, (hd + 1) * LANES)
        own = low_half if hd % 2 == 0 else jnp.logical_not(low_half)
        qaug_ref[:, cols] = (jnp.where(own, zq[:, pair], 0.0) + aug[:, cols]).astype(BF16)
        kaug_ref[:, cols] = (jnp.where(own, zk[:, pair], 0.0)
                             + aug[:, MIX_QW + hd * LANES:MIX_QW + (hd + 1) * LANES]).astype(BF16)


def _mix_in(h, g, w1, wvt, wa2, bfp, ba, batch, seq):
    tm = min(TOKEN_TILE, seq)
    nt = seq // tm
    tri = jnp.asarray(np.tril(np.ones((tm, tm), np.float32)), BF16)
    grp_np = np.arange(FOX_W)[:, None] // FOX_HEAD_DIM == np.arange(LANES)[None, :]
    grp = jnp.asarray(grp_np, BF16)
    grpt = jnp.asarray(grp_np[:, :16].T, BF16)
    fox_tk = min(FOX_TK, seq)
    row = lambda w: pl.BlockSpec((tm, w), lambda b, i: (b * nt + i, 0))
    t = batch * seq
    sds = jax.ShapeDtypeStruct
    return pl.pallas_call(
        functools.partial(_mix_in_kernel, fox_tk=fox_tk),
        out_shape=(sds((t, MIX_QW), BF16), sds((t, MIX_QW), BF16),
                   sds((batch, FOX_HEADS * FOX_VROWS, seq), BF16),
                   sds((t, GLA_KW), F32), sds((t, GLA_KW), F32), sds((t, GLA_VW), BF16),
                   sds((t, GLA_VW), F32), sds((t, GLA_KW), F32), sds((batch, FOX_HEADS, seq), F32),
                   sds((batch * nt, 8, LANES), F32)),
        grid=(batch, nt),
        in_specs=[row(D_MODEL), _const_spec((1, D_MODEL)), _const_spec(w1.shape), _const_spec(wvt.shape),
                  _const_spec(wa2.shape), _const_spec(bfp.shape), _const_spec(ba.shape),
                  _const_spec((LANES, 2 * MIX_QW)), _const_spec((tm, tm)), _const_spec((FOX_W, LANES)),
                  _const_spec((16, FOX_W))],
        out_specs=(row(MIX_QW), row(MIX_QW),
                   pl.BlockSpec((1, FOX_HEADS * FOX_VROWS, tm), lambda b, i: (b, 0, i)),
                   row(GLA_KW), row(GLA_KW), row(GLA_VW), row(GLA_VW), row(GLA_KW),
                   pl.BlockSpec((1, FOX_HEADS, tm), lambda b, i: (b, 0, i)),
                   pl.BlockSpec((1, 8, LANES), lambda b, i: (b * nt + i, 0, 0))),
        scratch_shapes=[pltpu.VMEM((2, LANES), F32), pltpu.VMEM((seq // fox_tk, LANES), F32)],
        compiler_params=pltpu.CompilerParams(dimension_semantics=("arbitrary", "arbitrary"),
                                             vmem_limit_bytes=VMEM_LIMIT_BYTES),
        name="mix_in",
    )(h, g, w1, wvt, wa2, bfp, ba, _aug_placement(), tri, grp, grpt)


def _fox_kernel(skip_ref, q_ref, k_ref, vt_ref, own_ref, mask_ref, mask2_ref, o_ref, p_ref, *, tq, tk, heads,
                nt):
    qi = pl.program_id(2)
    dh = FOX_HEAD_DIM
    vr = FOX_VROWS
    assert tq == tk

    def scores(hh, ks):
        cols = slice(hh * LANES, (hh + 1) * LANES)
        return _dot_nt(k_ref[pl.ds(ks, tk), cols], q_ref[:, cols])

    def exact_step(hh, j, carry, masked):
        m, l, acc = carry
        ks = pl.multiple_of(j * tk, tk)
        s = scores(hh, ks)
        if masked:
            s = s + mask_ref[...]
        m_new = jnp.maximum(m, jnp.max(s, axis=0, keepdims=True))
        p = jnp.exp2(s - m_new)
        alpha = jnp.exp2(m - m_new)
        l = alpha * l + jnp.sum(p, axis=0, keepdims=True)
        vt = vt_ref[0, hh * vr:hh * vr + dh, pl.ds(ks, tk)]
        acc = alpha * acc + _dot(vt, p.astype(BF16))
        return m_new, l, acc

    def weighted_values(hh, ks):
        return _dot(vt_ref[0, hh * vr:(hh + 1) * vr, pl.ds(ks, tk)], p_ref[hh])

    def lazy_block(ks, ks_prev, carry, masked):
        out = []
        for hh in range(heads):
            m, acc, over, shift_prev = carry[hh]
            s = scores(hh, ks)
            acc = (acc + weighted_values(hh, ks_prev)) * jnp.exp2(-shift_prev)
            p = jnp.exp2((s + mask_ref[...] if masked else s) - m).astype(BF16)
            p_ref[hh] = p
            over_b = jnp.log2(jnp.max(p, axis=0, keepdims=True).astype(F32))
            shift = jnp.maximum(over_b, 0.0)
            out.append((m + shift, acc, jnp.maximum(over, over_b), shift))
        return tuple(out)

    def finish(hh, l, acc):
        o_ref[0, hh * dh:(hh + 1) * dh, :] = (acc / l).astype(BF16)

    def redo():
        init = (jnp.full((1, tq), -jnp.inf, F32), jnp.zeros((1, tq), F32), jnp.zeros((dh, tq), F32))
        for hh in range(heads):
            c = exact_step(hh, qi, init, True)
            c = lax.fori_loop(0, qi, lambda j, c, hh=hh: exact_step(hh, j, c, False), c)
            finish(hh, c[1], c[2])

    kd = pl.multiple_of(qi * tk, tk)
    head0 = pl.program_id(1) * heads
    own = [own_ref[0, pl.ds(head0 + hh, 1), :] for hh in range(heads)]
    skip0 = (pl.program_id(0) * FOX_HEADS + head0) * nt + qi
    j0 = skip_ref[skip0]
    for hh in range(1, heads):
        j0 = jnp.minimum(j0, skip_ref[skip0 + hh * nt])
    one_before = jnp.logical_and(qi >= 1, j0 == qi - 1)

    @pl.when(one_before)
    def _():
        ks = pl.multiple_of(kd - tk, tk)
        cols = [slice(hh * LANES, (hh + 1) * LANES) for hh in range(heads)]
        s = [_dot_nt(k_ref[pl.ds(ks, 2 * tk), c], q_ref[:, c]) for c in cols]
        p = [jnp.exp2(x + mask2_ref[...] - m).astype(BF16) for x, m in zip(s, own)]
        acc = [_dot(vt_ref[0, hh * vr:(hh + 1) * vr, pl.ds(ks, 2 * tk)], p[hh]) for hh in range(heads)]
        over = [jnp.log2(jnp.max(x, axis=0, keepdims=True).astype(F32)) for x in p]
        for hh in range(heads):
            finish(hh, acc[hh][dh:dh + 1], acc[hh][:dh])

        @pl.when(jnp.logical_not(jnp.max(jnp.concatenate(over, axis=0)) <= FOX_LAZY_HEADROOM))
        def _():
            redo()

    @pl.when(jnp.logical_not(one_before))
    def _():
        zero = jnp.zeros((1, tq), F32)
        carry = tuple((own[hh], jnp.zeros((vr, tq), F32), zero, zero) for hh in range(heads))
        p_ref[...] = jnp.zeros_like(p_ref)

        def blocks(j, n, c):
            for i in range(n):
                prev = jnp.maximum(j + i - 1, 0)
                c = lazy_block(pl.multiple_of((j + i) * tk, tk), pl.multiple_of(prev * tk, tk), c, False)
            return c

        odd = (qi - j0) % 2
        carry = lax.fori_loop(0, odd, lambda j, c: blocks(j0 + j, 1, c), carry)
        carry = lax.fori_loop(0, (qi - j0) // 2, lambda t, c: blocks(j0 + odd + 2 * t, 2, c), carry)
        carry = lazy_block(kd, pl.multiple_of(jnp.maximum(qi - 1, 0) * tk, tk), carry, True)
        for hh in range(heads):
            acc = carry[hh][1] + weighted_values(hh, kd)
            finish(hh, acc[dh:dh + 1], acc[:dh])

        @pl.when(jnp.logical_not(jnp.max(jnp.concatenate([c[2] for c in carry], axis=0))
                                 <= FOX_LAZY_HEADROOM))
        def _():
            redo()


def _fox(qaug, kaug, vt, own, skip, batch, seq):
    tq = min(FOX_TQ, seq)
    tk = min(FOX_TK, tq)
    nq = seq // tq
    hp = FOX_HEADS_PER_STEP
    mask_np = np.where(np.arange(tk)[:, None] <= np.arange(tq)[None, :], 0.0, -np.inf)
    mask = jnp.asarray(mask_np, F32)
    mask2 = jnp.asarray(np.concatenate([np.zeros((tk, tq)), mask_np]), F32)
    nsub = min(TOKEN_TILE, seq) // tk
    table = skip[:, :nsub, :FOX_HEADS].reshape(batch, nq, FOX_HEADS).transpose(0, 2, 1)
    table = table.astype(jnp.int32).reshape(-1)
    tile_map = lambda f: (lambda b, h, i, tbl: f(b, h, i))
    return pl.pallas_call(
        functools.partial(_fox_kernel, tq=tq, tk=tk, heads=hp, nt=nq),
        out_shape=jax.ShapeDtypeStruct((batch, FOX_W, seq), BF16),
        grid_spec=pltpu.PrefetchScalarGridSpec(
            num_scalar_prefetch=1,
            grid=(batch, FOX_HEADS // hp, nq),
            in_specs=[pl.BlockSpec((tq, hp * LANES), tile_map(lambda b, h, i: (b * nq + i, h))),
                      pl.BlockSpec((seq, hp * LANES), tile_map(lambda b, h, i: (b, h))),
                      pl.BlockSpec((1, hp * FOX_VROWS, seq), tile_map(lambda b, h, i: (b, h, 0))),
                      pl.BlockSpec((1, FOX_HEADS, tq), tile_map(lambda b, h, i: (b, 0, i))),
                      pl.BlockSpec((tk, tq), tile_map(lambda b, h, i: (0, 0)),
                                   pipeline_mode=pl.Buffered(1)),
                      pl.BlockSpec((2 * tk, tq), tile_map(lambda b, h, i: (0, 0)),
                                   pipeline_mode=pl.Buffered(1))],
            out_specs=pl.BlockSpec((1, hp * FOX_HEAD_DIM, tq), tile_map(lambda b, h, i: (b, h, i))),
            scratch_shapes=[pltpu.VMEM((hp, tk, tq), BF16)]),
        compiler_params=pltpu.CompilerParams(dimension_semantics=("parallel", "parallel", "arbitrary"),
                                             vmem_limit_bytes=VMEM_LIMIT_BYTES),
        name="fox",
    )(table, qaug, kaug, vt, own, mask, mask2)


def _gla_kernel(gq_ref, gk_ref, gv_ref, gr_ref, la_ref, g_ref, tri_ref, o_ref, st_ref, *, nb, tile):
    @pl.when(pl.program_id(0) == 0)
    def _():
        st_ref[...] = jnp.zeros_like(st_ref)

    c = GLA_CHUNK
    lane_head = lax.broadcasted_iota(jnp.int32, (c, GLA_KW), 1) // GLA_HEAD_K
    st_head = lax.broadcasted_iota(jnp.int32, (GLA_HEAD_V, GLA_KW), 1) // GLA_HEAD_K
    causal = (lax.broadcasted_iota(jnp.int32, (c, c), 0) >= lax.broadcasted_iota(jnp.int32, (c, c), 1))
    tri = tri_ref[...]

    def chunks(ci, _):
        n = GLA_CHUNKS_PER_STEP
        rows = [pl.ds(pl.multiple_of((ci * n + i) * c, c), c) for i in range(n)]
        ib = [(i, b) for i in range(n) for b in range(nb)]
        ibh = [(i, b, h) for i, b in ib for h in range(GLA_HEADS)]
        vcols = lambda h: slice(h * GLA_HEAD_V, (h + 1) * GLA_HEAD_V)
        bc = {}
        for i, b in ib:
            la = la_ref[b, rows[i], :]
            la_hi = la.astype(BF16)
            la_lo = (la - la_hi.astype(F32)).astype(BF16)
            bc[i, b] = _dot(tri, la_hi) + _dot(tri, la_lo)
        qd, kd, ke, decay = {}, {}, {}, {}
        for i, b in ib:
            bl = bc[i, b][c - 1:c, :]
            k = gk_ref[b, rows[i], :]
            qd[i, b] = gq_ref[b, rows[i], :] * (GLA_HEAD_K ** -0.5) * jnp.exp(bc[i, b])
            kd[i, b] = (k * jnp.exp(-bc[i, b])).astype(BF16)
            ke[i, b] = (k * jnp.exp(bl - bc[i, b])).astype(BF16)
            decay[i, b] = jnp.exp(bl)
        qdh = {(i, b, h): jnp.where(lane_head == h, qd[i, b], 0.0).astype(BF16) for i, b, h in ibh}
        a = {(i, b, h): jnp.where(causal, _dot_nt(qdh[i, b, h], kd[i, b]), 0.0).astype(BF16)
             for i, b, h in ibh}
        vh = {(i, b, h): gv_ref[b, rows[i], vcols(h)] for i, b, h in ibh}
        o_intra = {(i, b, h): _dot(a[i, b, h], vh[i, b, h]) for i, b, h in ibh}
        u = {(i, b, h): _dot_tn(vh[i, b, h], ke[i, b]) for i, b, h in ibh}
        st = [st_ref[b] for b in range(nb)]
        for i in range(n):
            st_lo = [s.astype(BF16) for s in st]
            o = {(b, h): o_intra[i, b, h] + _dot_nt(qdh[i, b, h], st_lo[b])
                 for b in range(nb) for h in range(GLA_HEADS)}
            for (b, h), val in o.items():
                gr = gr_ref[b, rows[i], vcols(h)]
                o_ref[b, rows[i], vcols(h)] = (_rms_scale(val) * g_ref[:, vcols(h)]
                                               * (gr * _sigmoid(gr))).astype(BF16)
            st = [st[b] * decay[i, b] + sum(jnp.where(st_head == h, u[i, b, h], 0.0)
                                             for h in range(GLA_HEADS)) for b in range(nb)]
        for b in range(nb):
            st_ref[b] = st[b]
        return 0

    lax.fori_loop(0, tile // (c * GLA_CHUNKS_PER_STEP), chunks, 0)


def _gla(gq, gk, gv, gr, la, g_gla, batch, seq):
    tile = min(GLA_TILE, seq)
    tri = jnp.asarray(np.tril(np.ones((GLA_CHUNK, GLA_CHUNK), np.float32)), BF16)
    blk = lambda w: pl.BlockSpec((batch, tile, w), lambda i: (0, i, 0))
    r3 = lambda x: x.reshape(batch, seq, x.shape[-1])
    out = pl.pallas_call(
        functools.partial(_gla_kernel, nb=batch, tile=tile),
        out_shape=jax.ShapeDtypeStruct((batch, seq, GLA_VW), BF16),
        grid=(seq // tile,),
        in_specs=[blk(GLA_KW), blk(GLA_KW), blk(GLA_VW), blk(GLA_VW), blk(GLA_KW),
                  _const_spec((1, GLA_VW)), _const_spec((GLA_CHUNK, GLA_CHUNK))],
        out_specs=blk(GLA_VW),
        scratch_shapes=[pltpu.VMEM((batch, GLA_HEAD_V, GLA_KW), F32)],
        compiler_params=pltpu.CompilerParams(dimension_semantics=("arbitrary",),
                                             vmem_limit_bytes=VMEM_LIMIT_BYTES),
        name="gla",
    )(r3(gq), r3(gk), r3(gv), r3(gr), r3(la), g_gla, tri)
    return out.reshape(batch * seq, GLA_VW)


def _memkv_kernel(mem_ref, g_ref, w_ref, k_ref, v_ref):
    mn = (_rms_scale(mem_ref[0]) * g_ref[...]).astype(BF16)
    kv = _dot(mn, w_ref[...])
    k_ref[0] = kv[:, :D_MODEL].astype(BF16)
    v_ref[0] = kv[:, D_MODEL:].astype(BF16)


def _memkv(mem, g, w):
    batch = mem.shape[0]
    blk = pl.BlockSpec((1, N_MEM, D_MODEL), lambda b: (b, 0, 0))
    sds = jax.ShapeDtypeStruct((batch, N_MEM, D_MODEL), BF16)
    return pl.pallas_call(
        _memkv_kernel,
        out_shape=(sds, sds),
        grid=(batch,),
        in_specs=[blk, _const_spec((1, D_MODEL)), _const_spec(w.shape)],
        out_specs=(blk, blk),
        compiler_params=pltpu.CompilerParams(dimension_semantics=("parallel",),
                                             vmem_limit_bytes=VMEM_LIMIT_BYTES),
        name="memkv",
    )(mem, g, w)


def _post_kernel(h_ref, oft_ref, og_ref, wof_ref, wog_ref, g1_ref, g2_ref, wq_ref, km_ref, vm_ref,
                 wo_ref, g3_ref, out_ref):
    m = _dot_tn(oft_ref[0], wof_ref[...]) + _dot(og_ref[...], wog_ref[...])
    h2 = h_ref[...] + _rms_scale(m) * g1_ref[...]
    u = (_rms_scale(h2) * g2_ref[...]).astype(BF16)
    q = _dot(u, wq_ref[...]) * (MEM_HEAD_DIM ** -0.5)
    cols = [slice(hd * MEM_HEAD_DIM, (hd + 1) * MEM_HEAD_DIM) for hd in range(MEM_HEADS)]
    s = [_dot_nt(q[:, c].astype(BF16), km_ref[0, :, c]) for c in cols]
    p = [jnp.exp(x - jnp.max(x, axis=-1, keepdims=True)) for x in s]
    p = [(x / jnp.sum(x, axis=-1, keepdims=True)).astype(BF16) for x in p]
    o = jnp.concatenate([_dot(x, vm_ref[0, :, c]) for x, c in zip(p, cols)], axis=1).astype(BF16)
    c = _dot(o, wo_ref[...])
    out_ref[...] = h2 + _rms_scale(c) * g3_ref[...]


def _post(h, oft, og, wof, wog, g1, g2, wq, km, vm, wo, g3, batch, seq):
    tm = min(TOKEN_TILE, seq)
    nt = seq // tm
    row = lambda w: pl.BlockSpec((tm, w), lambda b, i: (b * nt + i, 0))
    memblk = pl.BlockSpec((1, N_MEM, D_MODEL), lambda b, i: (b, 0, 0))
    g = _const_spec((1, D_MODEL))
    return pl.pallas_call(
        _post_kernel,
        out_shape=jax.ShapeDtypeStruct((batch * seq, D_MODEL), F32),
        grid=(batch, nt),
        in_specs=[row(D_MODEL), pl.BlockSpec((1, FOX_W, tm), lambda b, i: (b, 0, i)), row(GLA_VW),
                  _const_spec(wof.shape), _const_spec(wog.shape), g, g, _const_spec(wq.shape),
                  memblk, memblk, _const_spec(wo.shape), g],
        out_specs=row(D_MODEL),
        compiler_params=pltpu.CompilerParams(dimension_semantics=("parallel", "parallel"),
                                             vmem_limit_bytes=VMEM_LIMIT_BYTES),
        name="post",
    )(h, oft, og, wof, wog, g1, g2, wq, km, vm, wo, g3)


def kernel(x, mem, ffn1_g_pre, ffn1_w_gu, ffn1_w_down, ffn1_g_post, mix_g_pre, mix_w_in, mix_w_a2, mix_b_a,
           mix_b_f, mix_g_gla, mix_w_out, mix_g_post, mem_g_pre, mem_g_kv, mem_w_q, mem_w_kv, mem_w_o,
           mem_g_post, ffn2_g_pre, ffn2_w_gu, ffn2_w_down, ffn2_g_post):
    batch, seq, _ = x.shape
    depth = ffn1_g_pre.shape[0]
    row = lambda v: v.reshape(1, -1)
    h = x.reshape(batch * seq, D_MODEL)
    for l in range(depth):
        wgu1, wd1 = _prep_ffn(ffn1_w_gu[l], ffn1_w_down[l])
        wgu2, wd2 = _prep_ffn(ffn2_w_gu[l], ffn2_w_down[l])
        w1, wvt, wa2, bfp, ba = _prep_mix(mix_w_in[l], mix_w_a2[l], mix_b_a[l], mix_b_f[l])
        w_out = mix_w_out[l].astype(BF16)

        h = _ffn(h, row(ffn1_g_pre[l]), wgu1, wd1, row(ffn1_g_post[l]))
        qaug, kaug, vt, gq, gk, gv, gr, la, own, skip = _mix_in(h, row(mix_g_pre[l]), w1, wvt, wa2, bfp, ba,
                                                                batch, seq)
        oft = _fox(qaug, kaug, vt, own, skip, batch, seq)
        og = _gla(gq, gk, gv, gr, la, row(mix_g_gla[l]), batch, seq)
        km, vm = _memkv(mem, row(mem_g_kv[l]), mem_w_kv[l].astype(BF16))
        h = _post(h, oft, og, w_out[:FOX_W], w_out[FOX_W:], row(mix_g_post[l]), row(mem_g_pre[l]),
                  mem_w_q[l].astype(BF16), km, vm, mem_w_o[l].astype(BF16), row(mem_g_post[l]), batch, seq)
        h = _ffn(h, row(ffn2_g_pre[l]), wgu2, wd2, row(ffn2_g_post[l]))
    return h.reshape(batch, seq, D_MODEL)
```

```python
import functools

import numpy as np
import jax
import jax.numpy as jnp
from jax import lax
from jax.experimental import pallas as pl
from jax.experimental.pallas import tpu as pltpu

F32 = jnp.float32
BF16 = jnp.bfloat16

D_MODEL = 1024
FOX_HEADS = 8
FOX_HEAD_DIM = 64
GLA_HEADS = 4
GLA_HEAD_K = 64
GLA_HEAD_V = 128
GLA_GATE_RANK = 16
GLA_GATE_TAU = 16.0
GLA_CHUNK = 64
FOX_W = FOX_HEADS * FOX_HEAD_DIM
GLA_KW = GLA_HEADS * GLA_HEAD_K
GLA_VW = GLA_HEADS * GLA_HEAD_V
IN_SPLITS = (FOX_W, FOX_W, FOX_W, FOX_HEADS, GLA_KW, GLA_KW, GLA_VW, GLA_VW, GLA_GATE_RANK)
N_MEM = 256
MEM_HEADS = 4
MEM_HEAD_DIM = D_MODEL // MEM_HEADS
D_FF = 2816
MACARON_W = 0.5
RMS_EPS = 1e-6
LOG2E = 1.4426950408889634

LANES = 128
VMEM_LIMIT_BYTES = 56 * 1024 * 1024

TOKEN_TILE = 512
FFN_TILE = 1024
FFN_SPLIT = 2
FFN_SIDE_ROWS = 64
FFN_CHUNK = 256
FOX_TQ = 256
FOX_TK = 256
FOX_HEADS_PER_STEP = 4
FOX_TILES_PER_STEP = 4
FOX_LAZY_HEADROOM = 60.0
FOX_SKIP_MARGIN = 160.0
FOX_NORM_SLACK = 1.02
FOX_VROWS = FOX_HEAD_DIM + 16
GLA_TILE = 512
GLA_CHUNKS_PER_STEP = 4

AUG_HI, AUG_MID, AUG_LO, AUG_ONE = 0, 8, 16, 24
GA_LANE = 8


def _rms_scale(x):
    return x * lax.rsqrt(jnp.mean(x * x, axis=-1, keepdims=True) + RMS_EPS)


def _sigmoid(x):
    return 1.0 / (1.0 + jnp.exp(-x))


def _log_sigmoid(x):
    return jnp.minimum(x, 0.0) - jnp.log(1.0 + jnp.exp(-jnp.abs(x)))


def _dot(a, b):
    return jnp.dot(a, b, preferred_element_type=F32)


def _dot_nt(a, b):
    return lax.dot_general(a, b, (((1,), (1,)), ((), ())), preferred_element_type=F32)


def _dot_tn(a, b):
    return lax.dot_general(a, b, (((0,), (0,)), ((), ())), preferred_element_type=F32)


def _split3(x):
    hi = x.astype(BF16)
    r1 = x - hi.astype(F32)
    mid = r1.astype(BF16)
    lo = (r1 - mid.astype(F32)).astype(BF16)
    return hi, mid, lo


def _const_spec(shape):
    zeros = (0,) * len(shape)
    return pl.BlockSpec(shape, lambda *_: zeros, pipeline_mode=pl.Buffered(1))


def _ffn_kernel(h_ref, gpre_ref, wgu_ref, wd_ref, gpost_ref, o_ref):
    n_chunks = D_FF // FFN_CHUNK
    rows = h_ref.shape[0] // FFN_SPLIT
    n_slices = min(n_chunks, rows // FFN_SIDE_ROWS)
    slices = [slice(s * (rows // n_slices), (s + 1) * (rows // n_slices)) for s in range(n_slices)]

    def shifted(sl, r):
        return slice(r * rows + sl.start, r * rows + sl.stop)

    def opening(sl):
        return (_rms_scale(h_ref[sl, :]) * gpre_ref[...]).astype(BF16)

    def closing(sl, acc_rows):
        o_ref[sl, :] = h_ref[sl, :] + MACARON_W * (_rms_scale(acc_rows) * gpost_ref[...])

    xn = opening(slice(0, rows))
    prev_acc = None
    for r in range(FFN_SPLIT):
        acc, nxt = None, []
        for j in range(n_chunks):
            gate = _dot(xn, wgu_ref[:, j * FFN_CHUNK:(j + 1) * FFN_CHUNK])
            up = _dot(xn, wgu_ref[:, D_FF + j * FFN_CHUNK:D_FF + (j + 1) * FFN_CHUNK])
            act = (gate * _sigmoid(gate) * up).astype(BF16)
            part = _dot(act, wd_ref[j * FFN_CHUNK:(j + 1) * FFN_CHUNK, :])
            acc = part if acc is None else acc + part
            if j < n_slices:
                if r + 1 < FFN_SPLIT:
                    nxt.append(opening(shifted(slices[j], r + 1)))
                if prev_acc is not None:
                    closing(shifted(slices[j], r - 1), prev_acc[slices[j], :])
        prev_acc = acc
        if nxt:
            xn = jnp.concatenate(nxt, axis=0)
    closing(slice((FFN_SPLIT - 1) * rows, FFN_SPLIT * rows), prev_acc)


def _prep_ffn(w_gu, w_down):
    return w_gu.astype(BF16), w_down.astype(BF16)


def _ffn(h, g_pre, wgu, wd, g_post):
    t = h.shape[0]
    tm = min(FFN_TILE, t)
    assert t % tm == 0
    row = pl.BlockSpec((tm, D_MODEL), lambda i: (i, 0))
    return pl.pallas_call(
        _ffn_kernel,
        out_shape=jax.ShapeDtypeStruct((t, D_MODEL), F32),
        grid=(t // tm,),
        in_specs=[row, _const_spec((1, D_MODEL)), _const_spec(wgu.shape), _const_spec(wd.shape),
                  _const_spec((1, D_MODEL))],
        out_specs=row,
        compiler_params=pltpu.CompilerParams(dimension_semantics=("parallel",),
                                             vmem_limit_bytes=VMEM_LIMIT_BYTES),
        name="ffn",
    )(h, g_pre, wgu, wd, g_post)


MIX_QW = FOX_HEADS * LANES
MIX_G0 = 2 * FOX_W
MIX_GW = 2 * GLA_KW + 2 * GLA_VW + LANES


def _prep_mix(w_in, w_a2, b_a, b_f):
    offs = np.cumsum(IN_SPLITS)[:-1].tolist()
    fq, fk, fv, fz, gq, gk, gv, gr, ga = jnp.split(w_in, offs, axis=1)
    pad = jnp.zeros((D_MODEL, LANES - FOX_HEADS - GLA_GATE_RANK), F32)
    w1 = jnp.concatenate([fq, fk, gq, gk, gv, gr, fz, ga, pad], axis=1).astype(BF16)
    wvt = fv.T.astype(BF16)
    wa2 = jnp.zeros((LANES, GLA_KW), F32).at[GA_LANE:GA_LANE + GLA_GATE_RANK].set(w_a2).astype(BF16)
    bfp = jnp.zeros((1, LANES), F32).at[0, :FOX_HEADS].set(b_f)
    return w1, wvt, wa2, bfp, b_a.reshape(1, GLA_KW)


def _aug_base(h):
    return FOX_HEAD_DIM if h % 2 == 0 else 0


def _aug_placement():
    p = np.zeros((LANES, 2 * MIX_QW), np.float32)
    for h in range(FOX_HEADS):
        qb = h * LANES + _aug_base(h)
        kb = MIX_QW + h * LANES + _aug_base(h)
        for i, src in enumerate((AUG_HI, AUG_MID, AUG_LO)):
            p[src + h, qb + i] = 1.0
            p[AUG_ONE, kb + i] = 1.0
            p[AUG_ONE, qb + 3 + i] = 1.0
            p[src + h, kb + 3 + i] = -1.0
    return jnp.asarray(p, BF16)


def _mix_in_kernel(h_ref, g_ref, w1_ref, wvt_ref, wa2_ref, bf_ref, ba_ref, p_ref, tri_ref, grp_ref, grpt_ref,
                   qaug_ref, kaug_ref, vt_ref, gq_ref, gk_ref, gv_ref, gr_ref, la_ref, own_ref, skip_ref,
                   carry_ref, hist_ref, *, fox_tk):
    @pl.when(pl.program_id(1) == 0)
    def _():
        carry_ref[...] = jnp.zeros_like(carry_ref)
        hist_ref[...] = jnp.full(hist_ref.shape, -jnp.inf, F32)

    tm = h_ref.shape[0]
    xn = (_rms_scale(h_ref[...]) * g_ref[...]).astype(BF16)
    zq = _dot(xn, w1_ref[:, :FOX_W]) * (FOX_HEAD_DIM ** -0.5 * LOG2E)
    zk = _dot(xn, w1_ref[:, FOX_W:MIX_G0])
    zg = _dot(xn, w1_ref[:, MIX_G0:])
    vt = _dot_nt(wvt_ref[...], xn).astype(BF16)
    for hd in range(FOX_HEADS):
        vt_ref[0, hd * FOX_VROWS:hd * FOX_VROWS + FOX_HEAD_DIM, :] = vt[hd * FOX_HEAD_DIM:(hd + 1) * FOX_HEAD_DIM]
        vt_ref[0, hd * FOX_VROWS + FOX_HEAD_DIM:(hd + 1) * FOX_VROWS, :] = jnp.ones(
            (FOX_VROWS - FOX_HEAD_DIM, tm), BF16)

    o = 0
    gq_ref[...] = zg[:, o:o + GLA_KW]; o += GLA_KW
    gk_ref[...] = zg[:, o:o + GLA_KW]; o += GLA_KW
    gv_ref[...] = zg[:, o:o + GLA_VW].astype(BF16); o += GLA_VW
    gr_ref[...] = zg[:, o:o + GLA_VW]; o += GLA_VW
    small = zg[:, o:o + LANES]

    xa = _dot(small.astype(BF16), wa2_ref[...]) + ba_ref[...]
    la_ref[...] = _log_sigmoid(xa) * (1.0 / GLA_GATE_TAU)

    lane = lax.broadcasted_iota(jnp.int32, (tm, LANES), 1)
    logf = jnp.where(lane < FOX_HEADS, _log_sigmoid(small + bf_ref[...]), 0.0)
    tri = tri_ref[...]
    l_hi, l_mid, l_lo = _split3(logf)
    l_all = (l_hi.astype(F32) + pltpu.roll(l_mid.astype(F32), AUG_MID, 1)
             + pltpu.roll(l_lo.astype(F32), AUG_LO, 1)).astype(BF16)
    part = _dot(tri, l_all)
    cum = part + pltpu.roll(part, LANES - AUG_MID, 1) + pltpu.roll(part, LANES - AUG_LO, 1)
    cum = jnp.where(lane < FOX_HEADS, cum, 0.0) + carry_ref[0:1, :]
    carry_ref[0:1, :] = cum[tm - 1:tm, :]

    c2 = cum * LOG2E
    own_ref[0] = _dot_nt(grpt_ref[...], (zq * zk).astype(BF16))[:FOX_HEADS]

    qn2 = _dot((zq * zq).astype(BF16), grp_ref[...])
    kn2 = _dot((zk * zk).astype(BF16), grp_ref[...])
    nsub = tm // fox_tk
    rows = []
    kn_run = carry_ref[1:2, :]
    for sb in range(nsub):
        r0 = sb * fox_tk
        kn_run = jnp.maximum(kn_run, jnp.sqrt(jnp.max(kn2[r0:r0 + fox_tk], axis=0, keepdims=True)))
        qn = jnp.sqrt(jnp.max(qn2[r0:r0 + fox_tk], axis=0, keepdims=True))
        reach = 2.0 * (FOX_NORM_SLACK * FOX_NORM_SLACK) * qn * kn_run + c2[r0:r0 + 1] + FOX_SKIP_MARGIN
        rows.append(jnp.sum(jnp.where(reach < hist_ref[...], 1.0, 0.0), axis=0, keepdims=True))
        hist_ref[pl.ds(pl.program_id(1) * nsub + sb, 1), :] = c2[r0 + fox_tk - 1:r0 + fox_tk]
    carry_ref[1:2, :] = kn_run
    skip_ref[0] = jnp.concatenate(rows + [jnp.zeros((8 - nsub, LANES), F32)], axis=0)

    c_hi, c_mid, c_lo = _split3(c2)
    packed = (c_hi.astype(F32) + pltpu.roll(c_mid.astype(F32), AUG_MID, 1)
              + pltpu.roll(c_lo.astype(F32), AUG_LO, 1))
    packed = jnp.where(lane == AUG_ONE, 1.0, packed)
    aug = _dot(packed.astype(BF16), p_ref[...])
    low_half = lane < FOX_HEAD_DIM
    for hd in range(FOX_HEADS):
        pair = slice((hd // 2) * LANES, (hd // 2 + 1) * LANES)
        cols = slice(hd * LANES, (hd + 1) * LANES)
        own = low_half if hd % 2 == 0 else jnp.logical_not(low_half)
        qaug_ref[:, cols] = (jnp.where(own, zq[:, pair], 0.0) + aug[:, cols]).astype(BF16)
        kaug_ref[:, cols] = (jnp.where(own, zk[:, pair], 0.0)
                             + aug[:, MIX_QW + hd * LANES:MIX_QW + (hd + 1) * LANES]).astype(BF16)


def _mix_in(h, g, w1, wvt, wa2, bfp, ba, batch, seq):
    tm = min(TOKEN_TILE, seq)
    nt = seq // tm
    tri = jnp.asarray(np.tril(np.ones((tm, tm), np.float32)), BF16)
    grp_np = np.arange(FOX_W)[:, None] // FOX_HEAD_DIM == np.arange(LANES)[None, :]
    grp = jnp.asarray(grp_np, BF16)
    grpt = jnp.asarray(grp_np[:, :16].T, BF16)
    fox_tk = min(FOX_TK, seq)
    row = lambda w: pl.BlockSpec((tm, w), lambda b, i: (b * nt + i, 0))
    t = batch * seq
    sds = jax.ShapeDtypeStruct
    return pl.pallas_call(
        functools.partial(_mix_in_kernel, fox_tk=fox_tk),
        out_shape=(sds((t, MIX_QW), BF16), sds((t, MIX_QW), BF16),
                   sds((batch, FOX_HEADS * FOX_VROWS, seq), BF16),
                   sds((t, GLA_KW), F32), sds((t, GLA_KW), F32), sds((t, GLA_VW), BF16),
                   sds((t, GLA_VW), F32), sds((t, GLA_KW), F32), sds((batch, FOX_HEADS, seq), F32),
                   sds((batch * nt, 8, LANES), F32)),
        grid=(batch, nt),
        in_specs=[row(D_MODEL), _const_spec((1, D_MODEL)), _const_spec(w1.shape), _const_spec(wvt.shape),
                  _const_spec(wa2.shape), _const_spec(bfp.shape), _const_spec(ba.shape),
                  _const_spec((LANES, 2 * MIX_QW)), _const_spec((tm, tm)), _const_spec((FOX_W, LANES)),
                  _const_spec((16, FOX_W))],
        out_specs=(row(MIX_QW), row(MIX_QW),
                   pl.BlockSpec((1, FOX_HEADS * FOX_VROWS, tm), lambda b, i: (b, 0, i)),
                   row(GLA_KW), row(GLA_KW), row(GLA_VW), row(GLA_VW), row(GLA_KW),
                   pl.BlockSpec((1, FOX_HEADS, tm), lambda b, i: (b, 0, i)),
                   pl.BlockSpec((1, 8, LANES), lambda b, i: (b * nt + i, 0, 0))),
        scratch_shapes=[pltpu.VMEM((2, LANES), F32), pltpu.VMEM((seq // fox_tk, LANES), F32)],
        compiler_params=pltpu.CompilerParams(dimension_semantics=("arbitrary", "arbitrary"),
                                             vmem_limit_bytes=VMEM_LIMIT_BYTES),
        name="mix_in",
    )(h, g, w1, wvt, wa2, bfp, ba, _aug_placement(), tri, grp, grpt)


def _fox_kernel(*refs, tiles, **params):
    def tile(sub, carry):
        _fox_tile(sub, *refs, tiles=tiles, **params)
        return carry

    lax.fori_loop(0, tiles, tile, 0)


def _fox_tile(sub, skip_ref, q_ref, k_ref, vt_ref, own_ref, mask_ref, mask2_ref, o_ref, p_ref, *, tq, tk, heads,
              nt, tiles):
    qi = pl.program_id(2) * tiles + sub
    qrows = pl.ds(pl.multiple_of(sub * tq, tq), tq)
    dh = FOX_HEAD_DIM
    vr = FOX_VROWS
    assert tq == tk

    def scores(hh, ks):
        cols = slice(hh * LANES, (hh + 1) * LANES)
        return _dot_nt(k_ref[pl.ds(ks, tk), cols], q_ref[qrows, cols])

    def exact_step(hh, j, carry, masked):
        m, l, acc = carry
        ks = pl.multiple_of(j * tk, tk)
        s = scores(hh, ks)
        if masked:
            s = s + mask_ref[...]
        m_new = jnp.maximum(m, jnp.max(s, axis=0, keepdims=True))
        p = jnp.exp2(s - m_new)
        alpha = jnp.exp2(m - m_new)
        l = alpha * l + jnp.sum(p, axis=0, keepdims=True)
        vt = vt_ref[0, hh * vr:hh * vr + dh, pl.ds(ks, tk)]
        acc = alpha * acc + _dot(vt, p.astype(BF16))
        return m_new, l, acc

    def weighted_values(hh, ks):
        return _dot(vt_ref[0, hh * vr:(hh + 1) * vr, pl.ds(ks, tk)], p_ref[hh])

    def lazy_block(ks, ks_prev, carry, masked):
        out = []
        for hh in range(heads):
            m, acc, over, shift_prev = carry[hh]
            s = scores(hh, ks)
            acc = (acc + weighted_values(hh, ks_prev)) * jnp.exp2(-shift_prev)
            p = jnp.exp2((s + mask_ref[...] if masked else s) - m).astype(BF16)
            p_ref[hh] = p
            over_b = jnp.log2(jnp.max(p, axis=0, keepdims=True).astype(F32))
            shift = jnp.maximum(over_b, 0.0)
            out.append((m + shift, acc, jnp.maximum(over, over_b), shift))
        return tuple(out)

    def finish(hh, l, acc):
        o_ref[0, hh * dh:(hh + 1) * dh, qrows] = (acc / l).astype(BF16)

    def redo():
        init = (jnp.full((1, tq), -jnp.inf, F32), jnp.zeros((1, tq), F32), jnp.zeros((dh, tq), F32))
        for hh in range(heads):
            c = exact_step(hh, qi, init, True)
            c = lax.fori_loop(0, qi, lambda j, c, hh=hh: exact_step(hh, j, c, False), c)
            finish(hh, c[1], c[2])

    kd = pl.multiple_of(qi * tk, tk)
    head0 = pl.program_id(1) * heads
    own = [own_ref[0, pl.ds(head0 + hh, 1), qrows] for hh in range(heads)]
    skip0 = (pl.program_id(0) * FOX_HEADS + head0) * nt + qi
    j0 = skip_ref[skip0]
    for hh in range(1, heads):
        j0 = jnp.minimum(j0, skip_ref[skip0 + hh * nt])
    one_before = jnp.logical_and(qi >= 1, j0 == qi - 1)

    @pl.when(one_before)
    def _():
        ks = pl.multiple_of(kd - tk, tk)
        cols = [slice(hh * LANES, (hh + 1) * LANES) for hh in range(heads)]
        s = [_dot_nt(k_ref[pl.ds(ks, 2 * tk), c], q_ref[qrows, c]) for c in cols]
        p = [jnp.exp2(x + mask2_ref[...] - m).astype(BF16) for x, m in zip(s, own)]
        acc = [_dot(vt_ref[0, hh * vr:(hh + 1) * vr, pl.ds(ks, 2 * tk)], p[hh]) for hh in range(heads)]
        over = [jnp.log2(jnp.max(x, axis=0, keepdims=True).astype(F32)) for x in p]
        for hh in range(heads):
            finish(hh, acc[hh][dh:dh + 1], acc[hh][:dh])

        @pl.when(jnp.logical_not(jnp.max(jnp.concatenate(over, axis=0)) <= FOX_LAZY_HEADROOM))
        def _():
            redo()

    @pl.when(jnp.logical_not(one_before))
    def _():
        zero = jnp.zeros((1, tq), F32)
        carry = tuple((own[hh], jnp.zeros((vr, tq), F32), zero, zero) for hh in range(heads))
        p_ref[...] = jnp.zeros_like(p_ref)

        def blocks(j, n, c):
            for i in range(n):
                prev = jnp.maximum(j + i - 1, 0)
                c = lazy_block(pl.multiple_of((j + i) * tk, tk), pl.multiple_of(prev * tk, tk), c, False)
            return c

        odd = (qi - j0) % 2
        carry = lax.fori_loop(0, odd, lambda j, c: blocks(j0 + j, 1, c), carry)
        carry = lax.fori_loop(0, (qi - j0) // 2, lambda t, c: blocks(j0 + odd + 2 * t, 2, c), carry)
        carry = lazy_block(kd, pl.multiple_of(jnp.maximum(qi - 1, 0) * tk, tk), carry, True)
        for hh in range(heads):
            acc = carry[hh][1] + weighted_values(hh, kd)
            finish(hh, acc[dh:dh + 1], acc[:dh])

        @pl.when(jnp.logical_not(jnp.max(jnp.concatenate([c[2] for c in carry], axis=0))
                                 <= FOX_LAZY_HEADROOM))
        def _():
            redo()


def _fox(qaug, kaug, vt, own, skip, batch, seq):
    tq = min(FOX_TQ, seq)
    tk = min(FOX_TK, tq)
    nq = seq // tq
    hp = FOX_HEADS_PER_STEP
    tiles = min(FOX_TILES_PER_STEP, nq)
    tqs = tq * tiles
    assert nq % tiles == 0
    mask_np = np.where(np.arange(tk)[:, None] <= np.arange(tq)[None, :], 0.0, -np.inf)
    mask = jnp.asarray(mask_np, F32)
    mask2 = jnp.asarray(np.concatenate([np.zeros((tk, tq)), mask_np]), F32)
    nsub = min(TOKEN_TILE, seq) // tk
    table = skip[:, :nsub, :FOX_HEADS].reshape(batch, nq, FOX_HEADS).transpose(0, 2, 1)
    table = table.astype(jnp.int32).reshape(-1)
    tile_map = lambda f: (lambda b, h, i, tbl: f(b, h, i))
    return pl.pallas_call(
        functools.partial(_fox_kernel, tq=tq, tk=tk, heads=hp, nt=nq, tiles=tiles),
        out_shape=jax.ShapeDtypeStruct((batch, FOX_W, seq), BF16),
        grid_spec=pltpu.PrefetchScalarGridSpec(
            num_scalar_prefetch=1,
            grid=(batch, FOX_HEADS // hp, nq // tiles),
            in_specs=[pl.BlockSpec((tqs, hp * LANES), tile_map(lambda b, h, i: (b * (nq // tiles) + i, h))),
                      pl.BlockSpec((seq, hp * LANES), tile_map(lambda b, h, i: (b, h))),
                      pl.BlockSpec((1, hp * FOX_VROWS, seq), tile_map(lambda b, h, i: (b, h, 0))),
                      pl.BlockSpec((1, FOX_HEADS, tqs), tile_map(lambda b, h, i: (b, 0, i))),
                      pl.BlockSpec((tk, tq), tile_map(lambda b, h, i: (0, 0)),
                                   pipeline_mode=pl.Buffered(1)),
                      pl.BlockSpec((2 * tk, tq), tile_map(lambda b, h, i: (0, 0)),
                                   pipeline_mode=pl.Buffered(1))],
            out_specs=pl.BlockSpec((1, hp * FOX_HEAD_DIM, tqs), tile_map(lambda b, h, i: (b, h, i))),
            scratch_shapes=[pltpu.VMEM((hp, tk, tq), BF16)]),
        compiler_params=pltpu.CompilerParams(dimension_semantics=("parallel", "parallel", "arbitrary"),
                                             vmem_limit_bytes=VMEM_LIMIT_BYTES),
        name="fox",
    )(table, qaug, kaug, vt, own, mask, mask2)


def _gla_kernel(gq_ref, gk_ref, gv_ref, gr_ref, la_ref, g_ref, tri_ref, o_ref, st_ref, *, nb, tile):
    @pl.when(pl.program_id(0) == 0)
    def _():
        st_ref[...] = jnp.zeros_like(st_ref)

    c = GLA_CHUNK
    lane_head = lax.broadcasted_iota(jnp.int32, (c, GLA_KW), 1) // GLA_HEAD_K
    st_head = lax.broadcasted_iota(jnp.int32, (GLA_HEAD_V, GLA_KW), 1) // GLA_HEAD_K
    causal = (lax.broadcasted_iota(jnp.int32, (c, c), 0) >= lax.broadcasted_iota(jnp.int32, (c, c), 1))
    tri = tri_ref[...]

    def chunks(ci, _):
        n = GLA_CHUNKS_PER_STEP
        rows = [pl.ds(pl.multiple_of((ci * n + i) * c, c), c) for i in range(n)]
        ib = [(i, b) for i in range(n) for b in range(nb)]
        ibh = [(i, b, h) for i, b in ib for h in range(GLA_HEADS)]
        vcols = lambda h: slice(h * GLA_HEAD_V, (h + 1) * GLA_HEAD_V)
        bc = {}
        for i, b in ib:
            la = la_ref[b, rows[i], :]
            la_hi = la.astype(BF16)
            la_lo = (la - la_hi.astype(F32)).astype(BF16)
            bc[i, b] = _dot(tri, la_hi) + _dot(tri, la_lo)
        qd, kd, ke, decay = {}, {}, {}, {}
        for i, b in ib:
            bl = bc[i, b][c - 1:c, :]
            k = gk_ref[b, rows[i], :]
            qd[i, b] = gq_ref[b, rows[i], :] * (GLA_HEAD_K ** -0.5) * jnp.exp(bc[i, b])
            kd[i, b] = (k * jnp.exp(-bc[i, b])).astype(BF16)
            ke[i, b] = (k * jnp.exp(bl - bc[i, b])).astype(BF16)
            decay[i, b] = jnp.exp(bl)
        qdh = {(i, b, h): jnp.where(lane_head == h, qd[i, b], 0.0).astype(BF16) for i, b, h in ibh}
        a = {(i, b, h): jnp.where(causal, _dot_nt(qdh[i, b, h], kd[i, b]), 0.0).astype(BF16)
             for i, b, h in ibh}
        vh = {(i, b, h): gv_ref[b, rows[i], vcols(h)] for i, b, h in ibh}
        o_intra = {(i, b, h): _dot(a[i, b, h], vh[i, b, h]) for i, b, h in ibh}
        u = {(i, b, h): _dot_tn(vh[i, b, h], ke[i, b]) for i, b, h in ibh}
        st = [st_ref[b] for b in range(nb)]
        for i in range(n):
            st_lo = [s.astype(BF16) for s in st]
            o = {(b, h): o_intra[i, b, h] + _dot_nt(qdh[i, b, h], st_lo[b])
                 for b in range(nb) for h in range(GLA_HEADS)}
            for (b, h), val in o.items():
                gr = gr_ref[b, rows[i], vcols(h)]
                o_ref[b, rows[i], vcols(h)] = (_rms_scale(val) * g_ref[:, vcols(h)]
                                               * (gr * _sigmoid(gr))).astype(BF16)
            st = [st[b] * decay[i, b] + sum(jnp.where(st_head == h, u[i, b, h], 0.0)
                                             for h in range(GLA_HEADS)) for b in range(nb)]
        for b in range(nb):
            st_ref[b] = st[b]
        return 0

    lax.fori_loop(0, tile // (c * GLA_CHUNKS_PER_STEP), chunks, 0)


def _gla(gq, gk, gv, gr, la, g_gla, batch, seq):
    tile = min(GLA_TILE, seq)
    tri = jnp.asarray(np.tril(np.ones((GLA_CHUNK, GLA_CHUNK), np.float32)), BF16)
    blk = lambda w: pl.BlockSpec((batch, tile, w), lambda i: (0, i, 0))
    r3 = lambda x: x.reshape(batch, seq, x.shape[-1])
    out = pl.pallas_call(
        functools.partial(_gla_kernel, nb=batch, tile=tile),
        out_shape=jax.ShapeDtypeStruct((batch, seq, GLA_VW), BF16),
        grid=(seq // tile,),
        in_specs=[blk(GLA_KW), blk(GLA_KW), blk(GLA_VW), blk(GLA_VW), blk(GLA_KW),
                  _const_spec((1, GLA_VW)), _const_spec((GLA_CHUNK, GLA_CHUNK))],
        out_specs=blk(GLA_VW),
        scratch_shapes=[pltpu.VMEM((batch, GLA_HEAD_V, GLA_KW), F32)],
        compiler_params=pltpu.CompilerParams(dimension_semantics=("arbitrary",),
                                             vmem_limit_bytes=VMEM_LIMIT_BYTES),
        name="gla",
    )(r3(gq), r3(gk), r3(gv), r3(gr), r3(la), g_gla, tri)
    return out.reshape(batch * seq, GLA_VW)


def _memkv_kernel(mem_ref, g_ref, w_ref, k_ref, v_ref):
    mn = (_rms_scale(mem_ref[0]) * g_ref[...]).astype(BF16)
    kv = _dot(mn, w_ref[...])
    k_ref[0] = kv[:, :D_MODEL].astype(BF16)
    v_ref[0] = kv[:, D_MODEL:].astype(BF16)


def _memkv(mem, g, w):
    batch = mem.shape[0]
    blk = pl.BlockSpec((1, N_MEM, D_MODEL), lambda b: (b, 0, 0))
    sds = jax.ShapeDtypeStruct((batch, N_MEM, D_MODEL), BF16)
    return pl.pallas_call(
        _memkv_kernel,
        out_shape=(sds, sds),
        grid=(batch,),
        in_specs=[blk, _const_spec((1, D_MODEL)), _const_spec(w.shape)],
        out_specs=(blk, blk),
        compiler_params=pltpu.CompilerParams(dimension_semantics=("parallel",),
                                             vmem_limit_bytes=VMEM_LIMIT_BYTES),
        name="memkv",
    )(mem, g, w)


def _post_kernel(h_ref, oft_ref, og_ref, wof_ref, wog_ref, g1_ref, g2_ref, wq_ref, km_ref, vm_ref,
                 wo_ref, g3_ref, out_ref):
    m = _dot_tn(oft_ref[0], wof_ref[...]) + _dot(og_ref[...], wog_ref[...])
    h2 = h_ref[...] + _rms_scale(m) * g1_ref[...]
    u = (_rms_scale(h2) * g2_ref[...]).astype(BF16)
    q = _dot(u, wq_ref[...]) * (MEM_HEAD_DIM ** -0.5)
    cols = [slice(hd * MEM_HEAD_DIM, (hd + 1) * MEM_HEAD_DIM) for hd in range(MEM_HEADS)]
    s = [_dot_nt(q[:, c].astype(BF16), km_ref[0, :, c]) for c in cols]
    p = [jnp.exp(x - jnp.max(x, axis=-1, keepdims=True)) for x in s]
    p = [(x / jnp.sum(x, axis=-1, keepdims=True)).astype(BF16) for x in p]
    o = jnp.concatenate([_dot(x, vm_ref[0, :, c]) for x, c in zip(p, cols)], axis=1).astype(BF16)
    c = _dot(o, wo_ref[...])
    out_ref[...] = h2 + _rms_scale(c) * g3_ref[...]


def _post(h, oft, og, wof, wog, g1, g2, wq, km, vm, wo, g3, batch, seq):
    tm = min(TOKEN_TILE, seq)
    nt = seq // tm
    row = lambda w: pl.BlockSpec((tm, w), lambda b, i: (b * nt + i, 0))
    memblk = pl.BlockSpec((1, N_MEM, D_MODEL), lambda b, i: (b, 0, 0))
    g = _const_spec((1, D_MODEL))
    return pl.pallas_call(
        _post_kernel,
        out_shape=jax.ShapeDtypeStruct((batch * seq, D_MODEL), F32),
        grid=(batch, nt),
        in_specs=[row(D_MODEL), pl.BlockSpec((1, FOX_W, tm), lambda b, i: (b, 0, i)), row(GLA_VW),
                  _const_spec(wof.shape), _const_spec(wog.shape), g, g, _const_spec(wq.shape),
                  memblk, memblk, _const_spec(wo.shape), g],
        out_specs=row(D_MODEL),
        compiler_params=pltpu.CompilerParams(dimension_semantics=("parallel", "parallel"),
                                             vmem_limit_bytes=VMEM_LIMIT_BYTES),
        name="post",
    )(h, oft, og, wof, wog, g1, g2, wq, km, vm, wo, g3)


def kernel(x, mem, ffn1_g_pre, ffn1_w_gu, ffn1_w_down, ffn1_g_post, mix_g_pre, mix_w_in, mix_w_a2, mix_b_a,
           mix_b_f, mix_g_gla, mix_w_out, mix_g_post, mem_g_pre, mem_g_kv, mem_w_q, mem_w_kv, mem_w_o,
           mem_g_post, ffn2_g_pre, ffn2_w_gu, ffn2_w_down, ffn2_g_post):
    batch, seq, _ = x.shape
    depth = ffn1_g_pre.shape[0]
    row = lambda v: v.reshape(1, -1)
    h = x.reshape(batch * seq, D_MODEL)
    for l in range(depth):
        wgu1, wd1 = _prep_ffn(ffn1_w_gu[l], ffn1_w_down[l])
        wgu2, wd2 = _prep_ffn(ffn2_w_gu[l], ffn2_w_down[l])
        w1, wvt, wa2, bfp, ba = _prep_mix(mix_w_in[l], mix_w_a2[l], mix_b_a[l], mix_b_f[l])
        w_out = mix_w_out[l].astype(BF16)

        h = _ffn(h, row(ffn1_g_pre[l]), wgu1, wd1, row(ffn1_g_post[l]))
        qaug, kaug, vt, gq, gk, gv, gr, la, own, skip = _mix_in(h, row(mix_g_pre[l]), w1, wvt, wa2, bfp, ba,
                                                                batch, seq)
        oft = _fox(qaug, kaug, vt, own, skip, batch, seq)
        og = _gla(gq, gk, gv, gr, la, row(mix_g_gla[l]), batch, seq)
        km, vm = _memkv(mem, row(mem_g_kv[l]), mem_w_kv[l].astype(BF16))
        h = _post(h, oft, og, w_out[:FOX_W], w_out[FOX_W:], row(mix_g_post[l]), row(mem_g_pre[l]),
                  mem_w_q[l].astype(BF16), km, vm, mem_w_o[l].astype(BF16), row(mem_g_post[l]), batch, seq)
        h = _ffn(h, row(ffn2_g_pre[l]), wgu2, wd2, row(ffn2_g_post[l]))
    return h.reshape(batch, seq, D_MODEL)
```

```python
import functools

import numpy as np
import jax
import jax.numpy as jnp
from jax import lax
from jax.experimental import pallas as pl
from jax.experimental.pallas import tpu as pltpu

F32 = jnp.float32
BF16 = jnp.bfloat16

D_MODEL = 1024
FOX_HEADS = 8
FOX_HEAD_DIM = 64
GLA_HEADS = 4
GLA_HEAD_K = 64
GLA_HEAD_V = 128
GLA_GATE_RANK = 16
GLA_GATE_TAU = 16.0
GLA_CHUNK = 64
FOX_W = FOX_HEADS * FOX_HEAD_DIM
GLA_KW = GLA_HEADS * GLA_HEAD_K
GLA_VW = GLA_HEADS * GLA_HEAD_V
IN_SPLITS = (FOX_W, FOX_W, FOX_W, FOX_HEADS, GLA_KW, GLA_KW, GLA_VW, GLA_VW, GLA_GATE_RANK)
N_MEM = 256
MEM_HEADS = 4
MEM_HEAD_DIM = D_MODEL // MEM_HEADS
D_FF = 2816
MACARON_W = 0.5
RMS_EPS = 1e-6
LOG2E = 1.4426950408889634

LANES = 128
VMEM_LIMIT_BYTES = 56 * 1024 * 1024

TOKEN_TILE = 512
POST_TILE = 1024
POST_SPLIT = 4
FFN_TILE = 1024
FFN_SPLIT = 2
FFN_SIDE_ROWS = 64
FFN_CHUNK = 256
FOX_TQ = 256
FOX_TK = 256
FOX_HEADS_PER_STEP = 4
FOX_TILES_PER_STEP = 16
FOX_LAZY_HEADROOM = 60.0
FOX_SKIP_MARGIN = 160.0
FOX_NORM_SLACK = 1.02
FOX_VROWS = FOX_HEAD_DIM + 16
GLA_TILE = 512
GLA_CHUNKS_PER_STEP = 4

AUG_HI, AUG_MID, AUG_LO, AUG_ONE = 0, 8, 16, 24
GA_LANE = 8


def _rms_scale(x):
    return x * lax.rsqrt(jnp.mean(x * x, axis=-1, keepdims=True) + RMS_EPS)


def _sigmoid(x):
    return 1.0 / (1.0 + jnp.exp(-x))


def _log_sigmoid(x):
    return jnp.minimum(x, 0.0) - jnp.log(1.0 + jnp.exp(-jnp.abs(x)))


def _dot(a, b):
    return jnp.dot(a, b, preferred_element_type=F32)


def _dot_nt(a, b):
    return lax.dot_general(a, b, (((1,), (1,)), ((), ())), preferred_element_type=F32)


def _dot_tn(a, b):
    return lax.dot_general(a, b, (((0,), (0,)), ((), ())), preferred_element_type=F32)


def _split3(x):
    hi = x.astype(BF16)
    r1 = x - hi.astype(F32)
    mid = r1.astype(BF16)
    lo = (r1 - mid.astype(F32)).astype(BF16)
    return hi, mid, lo


def _const_spec(shape):
    zeros = (0,) * len(shape)
    return pl.BlockSpec(shape, lambda *_: zeros, pipeline_mode=pl.Buffered(1))


def _ffn_kernel(h_ref, gpre_ref, wgu_ref, wd_ref, gpost_ref, o_ref):
    n_chunks = D_FF // FFN_CHUNK
    rows = h_ref.shape[0] // FFN_SPLIT
    n_slices = min(n_chunks, rows // FFN_SIDE_ROWS)
    slices = [slice(s * (rows // n_slices), (s + 1) * (rows // n_slices)) for s in range(n_slices)]

    def shifted(sl, r):
        return slice(r * rows + sl.start, r * rows + sl.stop)

    def opening(sl):
        return (_rms_scale(h_ref[sl, :]) * gpre_ref[...]).astype(BF16)

    def closing(sl, acc_rows):
        o_ref[sl, :] = h_ref[sl, :] + MACARON_W * (_rms_scale(acc_rows) * gpost_ref[...])

    xn = opening(slice(0, rows))
    prev_acc = None
    for r in range(FFN_SPLIT):
        acc, nxt = None, []
        for j in range(n_chunks):
            gate = _dot(xn, wgu_ref[:, j * FFN_CHUNK:(j + 1) * FFN_CHUNK])
            up = _dot(xn, wgu_ref[:, D_FF + j * FFN_CHUNK:D_FF + (j + 1) * FFN_CHUNK])
            act = (gate * _sigmoid(gate) * up).astype(BF16)
            part = _dot(act, wd_ref[j * FFN_CHUNK:(j + 1) * FFN_CHUNK, :])
            acc = part if acc is None else acc + part
            if j < n_slices:
                if r + 1 < FFN_SPLIT:
                    nxt.append(opening(shifted(slices[j], r + 1)))
                if prev_acc is not None:
                    closing(shifted(slices[j], r - 1), prev_acc[slices[j], :])
        prev_acc = acc
        if nxt:
            xn = jnp.concatenate(nxt, axis=0)
    closing(slice((FFN_SPLIT - 1) * rows, FFN_SPLIT * rows), prev_acc)


def _prep_ffn(w_gu, w_down):
    return w_gu.astype(BF16), w_down.astype(BF16)


def _ffn(h, g_pre, wgu, wd, g_post):
    t = h.shape[0]
    tm = min(FFN_TILE, t)
    assert t % tm == 0
    row = pl.BlockSpec((tm, D_MODEL), lambda i: (i, 0))
    return pl.pallas_call(
        _ffn_kernel,
        out_shape=jax.ShapeDtypeStruct((t, D_MODEL), F32),
        grid=(t // tm,),
        in_specs=[row, _const_spec((1, D_MODEL)), _const_spec(wgu.shape), _const_spec(wd.shape),
                  _const_spec((1, D_MODEL))],
        out_specs=row,
        compiler_params=pltpu.CompilerParams(dimension_semantics=("parallel",),
                                             vmem_limit_bytes=VMEM_LIMIT_BYTES),
        name="ffn",
    )(h, g_pre, wgu, wd, g_post)


MIX_QW = FOX_HEADS * LANES
MIX_G0 = 2 * FOX_W
MIX_GW = 2 * GLA_KW + 2 * GLA_VW + LANES


def _prep_mix(w_in, w_a2, b_a, b_f):
    offs = np.cumsum(IN_SPLITS)[:-1].tolist()
    fq, fk, fv, fz, gq, gk, gv, gr, ga = jnp.split(w_in, offs, axis=1)
    pad = jnp.zeros((D_MODEL, LANES - FOX_HEADS - GLA_GATE_RANK), F32)
    w1 = jnp.concatenate([fq, fk, gq, gk, gv, gr, fz, ga, pad], axis=1).astype(BF16)
    wvt = fv.T.astype(BF16)
    wa2 = jnp.zeros((LANES, GLA_KW), F32).at[GA_LANE:GA_LANE + GLA_GATE_RANK].set(w_a2).astype(BF16)
    bfp = jnp.zeros((1, LANES), F32).at[0, :FOX_HEADS].set(b_f)
    return w1, wvt, wa2, bfp, b_a.reshape(1, GLA_KW)


def _aug_base(h):
    return FOX_HEAD_DIM if h % 2 == 0 else 0


def _aug_placement():
    p = np.zeros((LANES, 2 * MIX_QW), np.float32)
    for h in range(FOX_HEADS):
        qb = h * LANES + _aug_base(h)
        kb = MIX_QW + h * LANES + _aug_base(h)
        for i, src in enumerate((AUG_HI, AUG_MID, AUG_LO)):
            p[src + h, qb + i] = 1.0
            p[AUG_ONE, kb + i] = 1.0
            p[AUG_ONE, qb + 3 + i] = 1.0
            p[src + h, kb + 3 + i] = -1.0
    return jnp.asarray(p, BF16)


def _mix_in_kernel(h_ref, g_ref, w1_ref, wvt_ref, wa2_ref, bf_ref, ba_ref, p_ref, tri_ref, grp_ref, grpt_ref,
                   qaug_ref, kaug_ref, vt_ref, gq_ref, gk_ref, gv_ref, gr_ref, la_ref, own_ref, skip_ref,
                   carry_ref, hist_ref, *, fox_tk):
    @pl.when(pl.program_id(1) == 0)
    def _():
        carry_ref[...] = jnp.zeros_like(carry_ref)
        hist_ref[...] = jnp.full(hist_ref.shape, -jnp.inf, F32)

    tm = h_ref.shape[0]
    xn = (_rms_scale(h_ref[...]) * g_ref[...]).astype(BF16)
    zq = _dot(xn, w1_ref[:, :FOX_W]) * (FOX_HEAD_DIM ** -0.5 * LOG2E)
    zk = _dot(xn, w1_ref[:, FOX_W:MIX_G0])
    zg = _dot(xn, w1_ref[:, MIX_G0:])
    vt = _dot_nt(wvt_ref[...], xn).astype(BF16)
    for hd in range(FOX_HEADS):
        vt_ref[0, hd * FOX_VROWS:hd * FOX_VROWS + FOX_HEAD_DIM, :] = vt[hd * FOX_HEAD_DIM:(hd + 1) * FOX_HEAD_DIM]
        vt_ref[0, hd * FOX_VROWS + FOX_HEAD_DIM:(hd + 1) * FOX_VROWS, :] = jnp.ones(
            (FOX_VROWS - FOX_HEAD_DIM, tm), BF16)

    o = 0
    gq_ref[...] = zg[:, o:o + GLA_KW]; o += GLA_KW
    gk_ref[...] = zg[:, o:o + GLA_KW]; o += GLA_KW
    gv_ref[...] = zg[:, o:o + GLA_VW].astype(BF16); o += GLA_VW
    gr_ref[...] = zg[:, o:o + GLA_VW]; o += GLA_VW
    small = zg[:, o:o + LANES]

    xa = _dot(small.astype(BF16), wa2_ref[...]) + ba_ref[...]
    la_ref[...] = _log_sigmoid(xa) * (1.0 / GLA_GATE_TAU)

    lane = lax.broadcasted_iota(jnp.int32, (tm, LANES), 1)
    logf = jnp.where(lane < FOX_HEADS, _log_sigmoid(small + bf_ref[...]), 0.0)
    tri = tri_ref[...]
    l_hi, l_mid, l_lo = _split3(logf)
    l_all = (l_hi.astype(F32) + pltpu.roll(l_mid.astype(F32), AUG_MID, 1)
             + pltpu.roll(l_lo.astype(F32), AUG_LO, 1)).astype(BF16)
    part = _dot(tri, l_all)
    cum = part + pltpu.roll(part, LANES - AUG_MID, 1) + pltpu.roll(part, LANES - AUG_LO, 1)
    cum = jnp.where(lane < FOX_HEADS, cum, 0.0) + carry_ref[0:1, :]
    carry_ref[0:1, :] = cum[tm - 1:tm, :]

    c2 = cum * LOG2E
    own_ref[0] = _dot_nt(grpt_ref[...], (zq * zk).astype(BF16))[:FOX_HEADS]

    qn2 = _dot((zq * zq).astype(BF16), grp_ref[...])
    kn2 = _dot((zk * zk).astype(BF16), grp_ref[...])
    nsub = tm // fox_tk
    rows = []
    kn_run = carry_ref[1:2, :]
    for sb in range(nsub):
        r0 = sb * fox_tk
        kn_run = jnp.maximum(kn_run, jnp.sqrt(jnp.max(kn2[r0:r0 + fox_tk], axis=0, keepdims=True)))
        qn = jnp.sqrt(jnp.max(qn2[r0:r0 + fox_tk], axis=0, keepdims=True))
        reach = 2.0 * (FOX_NORM_SLACK * FOX_NORM_SLACK) * qn * kn_run + c2[r0:r0 + 1] + FOX_SKIP_MARGIN
        rows.append(jnp.sum(jnp.where(reach < hist_ref[...], 1.0, 0.0), axis=0, keepdims=True))
        hist_ref[pl.ds(pl.program_id(1) * nsub + sb, 1), :] = c2[r0 + fox_tk - 1:r0 + fox_tk]
    carry_ref[1:2, :] = kn_run
    skip_ref[0] = jnp.concatenate(rows + [jnp.zeros((8 - nsub, LANES), F32)], axis=0)

    c_hi, c_mid, c_lo = _split3(c2)
    packed = (c_hi.astype(F32) + pltpu.roll(c_mid.astype(F32), AUG_MID, 1)
              + pltpu.roll(c_lo.astype(F32), AUG_LO, 1))
    packed = jnp.where(lane == AUG_ONE, 1.0, packed)
    aug = _dot(packed.astype(BF16), p_ref[...])
    low_half = lane < FOX_HEAD_DIM
    for hd in range(FOX_HEADS):
        pair = slice((hd // 2) * LANES, (hd // 2 + 1) * LANES)
        cols = slice(hd * LANES, (hd + 1) * LANES)
        own = low_half if hd % 2 == 0 else jnp.logical_not(low_half)
        qaug_ref[:, cols] = (jnp.where(own, zq[:, pair], 0.0) + aug[:, cols]).astype(BF16)
        kaug_ref[:, cols] = (jnp.where(own, zk[:, pair], 0.0)
                             + aug[:, MIX_QW + hd * LANES:MIX_QW + (hd + 1) * LANES]).astype(BF16)


def _mix_in(h, g, w1, wvt, wa2, bfp, ba, batch, seq):
    tm = min(TOKEN_TILE, seq)
    nt = seq // tm
    tri = jnp.asarray(np.tril(np.ones((tm, tm), np.float32)), BF16)
    grp_np = np.arange(FOX_W)[:, None] // FOX_HEAD_DIM == np.arange(LANES)[None, :]
    grp = jnp.asarray(grp_np, BF16)
    grpt = jnp.asarray(grp_np[:, :16].T, BF16)
    fox_tk = min(FOX_TK, seq)
    row = lambda w: pl.BlockSpec((tm, w), lambda b, i: (b * nt + i, 0))
    t = batch * seq
    sds = jax.ShapeDtypeStruct
    return pl.pallas_call(
        functools.partial(_mix_in_kernel, fox_tk=fox_tk),
        out_shape=(sds((t, MIX_QW), BF16), sds((t, MIX_QW), BF16),
                   sds((batch, FOX_HEADS * FOX_VROWS, seq), BF16),
                   sds((t, GLA_KW), F32), sds((t, GLA_KW), F32), sds((t, GLA_VW), BF16),
                   sds((t, GLA_VW), F32), sds((t, GLA_KW), F32), sds((batch, FOX_HEADS, seq), F32),
                   sds((batch * nt, 8, LANES), F32)),
        grid=(batch, nt),
        in_specs=[row(D_MODEL), _const_spec((1, D_MODEL)), _const_spec(w1.shape), _const_spec(wvt.shape),
                  _const_spec(wa2.shape), _const_spec(bfp.shape), _const_spec(ba.shape),
                  _const_spec((LANES, 2 * MIX_QW)), _const_spec((tm, tm)), _const_spec((FOX_W, LANES)),
                  _const_spec((16, FOX_W))],
        out_specs=(row(MIX_QW), row(MIX_QW),
                   pl.BlockSpec((1, FOX_HEADS * FOX_VROWS, tm), lambda b, i: (b, 0, i)),
                   row(GLA_KW), row(GLA_KW), row(GLA_VW), row(GLA_VW), row(GLA_KW),
                   pl.BlockSpec((1, FOX_HEADS, tm), lambda b, i: (b, 0, i)),
                   pl.BlockSpec((1, 8, LANES), lambda b, i: (b * nt + i, 0, 0))),
        scratch_shapes=[pltpu.VMEM((2, LANES), F32), pltpu.VMEM((seq // fox_tk, LANES), F32)],
        compiler_params=pltpu.CompilerParams(dimension_semantics=("arbitrary", "arbitrary"),
                                             vmem_limit_bytes=VMEM_LIMIT_BYTES),
        name="mix_in",
    )(h, g, w1, wvt, wa2, bfp, ba, _aug_placement(), tri, grp, grpt)


def _fox_kernel(*refs, tiles, **params):
    def tile(sub, carry):
        _fox_tile(sub, *refs, tiles=tiles, **params)
        return carry

    lax.fori_loop(0, tiles, tile, 0)


def _fox_tile(sub, skip_ref, q_ref, k_ref, vt_ref, own_ref, mask_ref, mask2_ref, o_ref, p_ref, *, tq, tk, heads,
              nt, tiles):
    qi = pl.program_id(2) * tiles + sub
    qrows = pl.ds(pl.multiple_of(sub * tq, tq), tq)
    dh = FOX_HEAD_DIM
    vr = FOX_VROWS
    assert tq == tk

    def scores(hh, ks):
        cols = slice(hh * LANES, (hh + 1) * LANES)
        return _dot_nt(k_ref[pl.ds(ks, tk), cols], q_ref[qrows, cols])

    def exact_step(hh, j, carry, masked):
        m, l, acc = carry
        ks = pl.multiple_of(j * tk, tk)
        s = scores(hh, ks)
        if masked:
            s = s + mask_ref[...]
        m_new = jnp.maximum(m, jnp.max(s, axis=0, keepdims=True))
        p = jnp.exp2(s - m_new)
        alpha = jnp.exp2(m - m_new)
        l = alpha * l + jnp.sum(p, axis=0, keepdims=True)
        vt = vt_ref[0, hh * vr:hh * vr + dh, pl.ds(ks, tk)]
        acc = alpha * acc + _dot(vt, p.astype(BF16))
        return m_new, l, acc

    def weighted_values(hh, ks):
        return _dot(vt_ref[0, hh * vr:(hh + 1) * vr, pl.ds(ks, tk)], p_ref[hh])

    def lazy_block(ks, ks_prev, carry, masked):
        out = []
        for hh in range(heads):
            m, acc, over, shift_prev = carry[hh]
            s = scores(hh, ks)
            acc = (acc + weighted_values(hh, ks_prev)) * jnp.exp2(-shift_prev)
            p = jnp.exp2((s + mask_ref[...] if masked else s) - m).astype(BF16)
            p_ref[hh] = p
            over_b = jnp.log2(jnp.max(p, axis=0, keepdims=True).astype(F32))
            shift = jnp.maximum(over_b, 0.0)
            out.append((m + shift, acc, jnp.maximum(over, over_b), shift))
        return tuple(out)

    def finish(hh, l, acc):
        o_ref[0, hh * dh:(hh + 1) * dh, qrows] = (acc / l).astype(BF16)

    def redo():
        init = (jnp.full((1, tq), -jnp.inf, F32), jnp.zeros((1, tq), F32), jnp.zeros((dh, tq), F32))
        for hh in range(heads):
            c = exact_step(hh, qi, init, True)
            c = lax.fori_loop(0, qi, lambda j, c, hh=hh: exact_step(hh, j, c, False), c)
            finish(hh, c[1], c[2])

    kd = pl.multiple_of(qi * tk, tk)
    head0 = pl.program_id(1) * heads
    own = [own_ref[0, pl.ds(head0 + hh, 1), qrows] for hh in range(heads)]
    skip0 = (pl.program_id(0) * FOX_HEADS + head0) * nt + qi
    j0 = skip_ref[skip0]
    for hh in range(1, heads):
        j0 = jnp.minimum(j0, skip_ref[skip0 + hh * nt])
    one_before = jnp.logical_and(qi >= 1, j0 == qi - 1)

    @pl.when(one_before)
    def _():
        ks = pl.multiple_of(kd - tk, tk)
        cols = [slice(hh * LANES, (hh + 1) * LANES) for hh in range(heads)]
        s = [_dot_nt(k_ref[pl.ds(ks, 2 * tk), c], q_ref[qrows, c]) for c in cols]
        p = [jnp.exp2(x + mask2_ref[...] - m).astype(BF16) for x, m in zip(s, own)]
        acc = [_dot(vt_ref[0, hh * vr:(hh + 1) * vr, pl.ds(ks, 2 * tk)], p[hh]) for hh in range(heads)]
        over = [jnp.log2(jnp.max(x, axis=0, keepdims=True).astype(F32)) for x in p]
        for hh in range(heads):
            finish(hh, acc[hh][dh:dh + 1], acc[hh][:dh])

        @pl.when(jnp.logical_not(jnp.max(jnp.concatenate(over, axis=0)) <= FOX_LAZY_HEADROOM))
        def _():
            redo()

    @pl.when(jnp.logical_not(one_before))
    def _():
        zero = jnp.zeros((1, tq), F32)
        carry = tuple((own[hh], jnp.zeros((vr, tq), F32), zero, zero) for hh in range(heads))
        p_ref[...] = jnp.zeros_like(p_ref)

        def blocks(j, n, c):
            for i in range(n):
                prev = jnp.maximum(j + i - 1, 0)
                c = lazy_block(pl.multiple_of((j + i) * tk, tk), pl.multiple_of(prev * tk, tk), c, False)
            return c

        odd = (qi - j0) % 2
        carry = lax.fori_loop(0, odd, lambda j, c: blocks(j0 + j, 1, c), carry)
        carry = lax.fori_loop(0, (qi - j0) // 2, lambda t, c: blocks(j0 + odd + 2 * t, 2, c), carry)
        carry = lazy_block(kd, pl.multiple_of(jnp.maximum(qi - 1, 0) * tk, tk), carry, True)
        for hh in range(heads):
            acc = carry[hh][1] + weighted_values(hh, kd)
            finish(hh, acc[dh:dh + 1], acc[:dh])

        @pl.when(jnp.logical_not(jnp.max(jnp.concatenate([c[2] for c in carry], axis=0))
                                 <= FOX_LAZY_HEADROOM))
        def _():
            redo()


def _fox(qaug, kaug, vt, own, skip, batch, seq):
    tq = min(FOX_TQ, seq)
    tk = min(FOX_TK, tq)
    nq = seq // tq
    hp = FOX_HEADS_PER_STEP
    tiles = min(FOX_TILES_PER_STEP, nq)
    tqs = tq * tiles
    assert nq % tiles == 0
    mask_np = np.where(np.arange(tk)[:, None] <= np.arange(tq)[None, :], 0.0, -np.inf)
    mask = jnp.asarray(mask_np, F32)
    mask2 = jnp.asarray(np.concatenate([np.zeros((tk, tq)), mask_np]), F32)
    nsub = min(TOKEN_TILE, seq) // tk
    table = skip[:, :nsub, :FOX_HEADS].reshape(batch, nq, FOX_HEADS).transpose(0, 2, 1)
    table = table.astype(jnp.int32).reshape(-1)
    tile_map = lambda f: (lambda b, h, i, tbl: f(b, h, i))
    return pl.pallas_call(
        functools.partial(_fox_kernel, tq=tq, tk=tk, heads=hp, nt=nq, tiles=tiles),
        out_shape=jax.ShapeDtypeStruct((batch, FOX_W, seq), BF16),
        grid_spec=pltpu.PrefetchScalarGridSpec(
            num_scalar_prefetch=1,
            grid=(batch, FOX_HEADS // hp, nq // tiles),
            in_specs=[pl.BlockSpec((tqs, hp * LANES), tile_map(lambda b, h, i: (b * (nq // tiles) + i, h))),
                      pl.BlockSpec((seq, hp * LANES), tile_map(lambda b, h, i: (b, h))),
                      pl.BlockSpec((1, hp * FOX_VROWS, seq), tile_map(lambda b, h, i: (b, h, 0))),
                      pl.BlockSpec((1, FOX_HEADS, tqs), tile_map(lambda b, h, i: (b, 0, i))),
                      pl.BlockSpec((tk, tq), tile_map(lambda b, h, i: (0, 0)),
                                   pipeline_mode=pl.Buffered(1)),
                      pl.BlockSpec((2 * tk, tq), tile_map(lambda b, h, i: (0, 0)),
                                   pipeline_mode=pl.Buffered(1))],
            out_specs=pl.BlockSpec((1, hp * FOX_HEAD_DIM, tqs), tile_map(lambda b, h, i: (b, h, i))),
            scratch_shapes=[pltpu.VMEM((hp, tk, tq), BF16)]),
        compiler_params=pltpu.CompilerParams(dimension_semantics=("parallel", "parallel", "arbitrary"),
                                             vmem_limit_bytes=VMEM_LIMIT_BYTES),
        name="fox",
    )(table, qaug, kaug, vt, own, mask, mask2)


def _gla_kernel(gq_ref, gk_ref, gv_ref, gr_ref, la_ref, g_ref, tri_ref, o_ref, st_ref, *, nb, tile):
    @pl.when(pl.program_id(0) == 0)
    def _():
        st_ref[...] = jnp.zeros_like(st_ref)

    c = GLA_CHUNK
    lane_head = lax.broadcasted_iota(jnp.int32, (c, GLA_KW), 1) // GLA_HEAD_K
    st_head = lax.broadcasted_iota(jnp.int32, (GLA_HEAD_V, GLA_KW), 1) // GLA_HEAD_K
    causal = (lax.broadcasted_iota(jnp.int32, (c, c), 0) >= lax.broadcasted_iota(jnp.int32, (c, c), 1))
    tri = tri_ref[...]

    def chunks(ci, _):
        n = GLA_CHUNKS_PER_STEP
        rows = [pl.ds(pl.multiple_of((ci * n + i) * c, c), c) for i in range(n)]
        ib = [(i, b) for i in range(n) for b in range(nb)]
        ibh = [(i, b, h) for i, b in ib for h in range(GLA_HEADS)]
        vcols = lambda h: slice(h * GLA_HEAD_V, (h + 1) * GLA_HEAD_V)
        bc = {}
        for i, b in ib:
            la = la_ref[b, rows[i], :]
            la_hi = la.astype(BF16)
            la_lo = (la - la_hi.astype(F32)).astype(BF16)
            bc[i, b] = _dot(tri, la_hi) + _dot(tri, la_lo)
        qd, kd, ke, decay = {}, {}, {}, {}
        for i, b in ib:
            bl = bc[i, b][c - 1:c, :]
            k = gk_ref[b, rows[i], :]
            qd[i, b] = gq_ref[b, rows[i], :] * (GLA_HEAD_K ** -0.5) * jnp.exp(bc[i, b])
            kd[i, b] = (k * jnp.exp(-bc[i, b])).astype(BF16)
            ke[i, b] = (k * jnp.exp(bl - bc[i, b])).astype(BF16)
            decay[i, b] = jnp.exp(bl)
        qdh = {(i, b, h): jnp.where(lane_head == h, qd[i, b], 0.0).astype(BF16) for i, b, h in ibh}
        a = {(i, b, h): jnp.where(causal, _dot_nt(qdh[i, b, h], kd[i, b]), 0.0).astype(BF16)
             for i, b, h in ibh}
        vh = {(i, b, h): gv_ref[b, rows[i], vcols(h)] for i, b, h in ibh}
        o_intra = {(i, b, h): _dot(a[i, b, h], vh[i, b, h]) for i, b, h in ibh}
        u = {(i, b, h): _dot_tn(vh[i, b, h], ke[i, b]) for i, b, h in ibh}
        st = [st_ref[b] for b in range(nb)]
        for i in range(n):
            st_lo = [s.astype(BF16) for s in st]
            o = {(b, h): o_intra[i, b, h] + _dot_nt(qdh[i, b, h], st_lo[b])
                 for b in range(nb) for h in range(GLA_HEADS)}
            for (b, h), val in o.items():
                gr = gr_ref[b, rows[i], vcols(h)]
                o_ref[b, rows[i], vcols(h)] = (_rms_scale(val) * g_ref[:, vcols(h)]
                                               * (gr * _sigmoid(gr))).astype(BF16)
            st = [st[b] * decay[i, b] + sum(jnp.where(st_head == h, u[i, b, h], 0.0)
                                             for h in range(GLA_HEADS)) for b in range(nb)]
        for b in range(nb):
            st_ref[b] = st[b]
        return 0

    lax.fori_loop(0, tile // (c * GLA_CHUNKS_PER_STEP), chunks, 0)


def _gla(gq, gk, gv, gr, la, g_gla, batch, seq):
    tile = min(GLA_TILE, seq)
    tri = jnp.asarray(np.tril(np.ones((GLA_CHUNK, GLA_CHUNK), np.float32)), BF16)
    blk = lambda w: pl.BlockSpec((batch, tile, w), lambda i: (0, i, 0))
    r3 = lambda x: x.reshape(batch, seq, x.shape[-1])
    out = pl.pallas_call(
        functools.partial(_gla_kernel, nb=batch, tile=tile),
        out_shape=jax.ShapeDtypeStruct((batch, seq, GLA_VW), BF16),
        grid=(seq // tile,),
        in_specs=[blk(GLA_KW), blk(GLA_KW), blk(GLA_VW), blk(GLA_VW), blk(GLA_KW),
                  _const_spec((1, GLA_VW)), _const_spec((GLA_CHUNK, GLA_CHUNK))],
        out_specs=blk(GLA_VW),
        scratch_shapes=[pltpu.VMEM((batch, GLA_HEAD_V, GLA_KW), F32)],
        compiler_params=pltpu.CompilerParams(dimension_semantics=("arbitrary",),
                                             vmem_limit_bytes=VMEM_LIMIT_BYTES),
        name="gla",
    )(r3(gq), r3(gk), r3(gv), r3(gr), r3(la), g_gla, tri)
    return out.reshape(batch * seq, GLA_VW)


def _memkv_kernel(mem_ref, g_ref, w_ref, k_ref, v_ref):
    mn = (_rms_scale(mem_ref[0]) * g_ref[...]).astype(BF16)
    kv = _dot(mn, w_ref[...])
    k_ref[0] = kv[:, :D_MODEL].astype(BF16)
    v_ref[0] = kv[:, D_MODEL:].astype(BF16)


def _memkv(mem, g, w):
    batch = mem.shape[0]
    blk = pl.BlockSpec((1, N_MEM, D_MODEL), lambda b: (b, 0, 0))
    sds = jax.ShapeDtypeStruct((batch, N_MEM, D_MODEL), BF16)
    return pl.pallas_call(
        _memkv_kernel,
        out_shape=(sds, sds),
        grid=(batch,),
        in_specs=[blk, _const_spec((1, D_MODEL)), _const_spec(w.shape)],
        out_specs=(blk, blk),
        compiler_params=pltpu.CompilerParams(dimension_semantics=("parallel",),
                                             vmem_limit_bytes=VMEM_LIMIT_BYTES),
        name="memkv",
    )(mem, g, w)


def _post_kernel(h_ref, oft_ref, og_ref, wof_ref, wog_ref, g1_ref, g2_ref, wq_ref, km_ref, vm_ref,
                 wo_ref, g3_ref, out_ref):
    rows = h_ref.shape[0] // POST_SPLIT
    groups = [slice(r * rows, (r + 1) * rows) for r in range(POST_SPLIT)]
    cols = [slice(hd * MEM_HEAD_DIM, (hd + 1) * MEM_HEAD_DIM) for hd in range(MEM_HEADS)]
    m = [_dot_tn(oft_ref[0, :, g], wof_ref[...]) + _dot(og_ref[g, :], wog_ref[...]) for g in groups]
    h2 = [h_ref[g, :] + _rms_scale(x) * g1_ref[...] for g, x in zip(groups, m)]
    u = [(_rms_scale(x) * g2_ref[...]).astype(BF16) for x in h2]
    q = [_dot(x, wq_ref[...]) * (MEM_HEAD_DIM ** -0.5) for x in u]
    s = [[_dot_nt(x[:, c].astype(BF16), km_ref[0, :, c]) for c in cols] for x in q]
    p = [[jnp.exp(y - jnp.max(y, axis=-1, keepdims=True)) for y in x] for x in s]
    p = [[(y / jnp.sum(y, axis=-1, keepdims=True)).astype(BF16) for y in x] for x in p]
    o = [jnp.concatenate([_dot(y, vm_ref[0, :, c]) for y, c in zip(x, cols)], axis=1).astype(BF16) for x in p]
    c = [_dot(x, wo_ref[...]) for x in o]
    for g, x, y in zip(groups, h2, c):
        out_ref[g, :] = x + _rms_scale(y) * g3_ref[...]


def _post(h, oft, og, wof, wog, g1, g2, wq, km, vm, wo, g3, batch, seq):
    tm = min(POST_TILE, seq)
    nt = seq // tm
    row = lambda w: pl.BlockSpec((tm, w), lambda b, i: (b * nt + i, 0))
    memblk = pl.BlockSpec((1, N_MEM, D_MODEL), lambda b, i: (b, 0, 0))
    g = _const_spec((1, D_MODEL))
    return pl.pallas_call(
        _post_kernel,
        out_shape=jax.ShapeDtypeStruct((batch * seq, D_MODEL), F32),
        grid=(batch, nt),
        in_specs=[row(D_MODEL), pl.BlockSpec((1, FOX_W, tm), lambda b, i: (b, 0, i)), row(GLA_VW),
                  _const_spec(wof.shape), _const_spec(wog.shape), g, g, _const_spec(wq.shape),
                  memblk, memblk, _const_spec(wo.shape), g],
        out_specs=row(D_MODEL),
        compiler_params=pltpu.CompilerParams(dimension_semantics=("parallel", "parallel"),
                                             vmem_limit_bytes=VMEM_LIMIT_BYTES),
        name="post",
    )(h, oft, og, wof, wog, g1, g2, wq, km, vm, wo, g3)


def kernel(x, mem, ffn1_g_pre, ffn1_w_gu, ffn1_w_down, ffn1_g_post, mix_g_pre, mix_w_in, mix_w_a2, mix_b_a,
           mix_b_f, mix_g_gla, mix_w_out, mix_g_post, mem_g_pre, mem_g_kv, mem_w_q, mem_w_kv, mem_w_o,
           mem_g_post, ffn2_g_pre, ffn2_w_gu, ffn2_w_down, ffn2_g_post):
    batch, seq, _ = x.shape
    depth = ffn1_g_pre.shape[0]
    row = lambda v: v.reshape(1, -1)
    h = x.reshape(batch * seq, D_MODEL)
    for l in range(depth):
        wgu1, wd1 = _prep_ffn(ffn1_w_gu[l], ffn1_w_down[l])
        wgu2, wd2 = _prep_ffn(ffn2_w_gu[l], ffn2_w_down[l])
        w1, wvt, wa2, bfp, ba = _prep_mix(mix_w_in[l], mix_w_a2[l], mix_b_a[l], mix_b_f[l])
        w_out = mix_w_out[l].astype(BF16)

        h = _ffn(h, row(ffn1_g_pre[l]), wgu1, wd1, row(ffn1_g_post[l]))
        qaug, kaug, vt, gq, gk, gv, gr, la, own, skip = _mix_in(h, row(mix_g_pre[l]), w1, wvt, wa2, bfp, ba,
                                                                batch, seq)
        oft = _fox(qaug, kaug, vt, own, skip, batch, seq)
        og = _gla(gq, gk, gv, gr, la, row(mix_g_gla[l]), batch, seq)
        km, vm = _memkv(mem, row(mem_g_kv[l]), mem_w_kv[l].astype(BF16))
        h = _post(h, oft, og, w_out[:FOX_W], w_out[FOX_W:], row(mix_g_post[l]), row(mem_g_pre[l]),
                  mem_w_q[l].astype(BF16), km, vm, mem_w_o[l].astype(BF16), row(mem_g_post[l]), batch, seq)
        h = _ffn(h, row(ffn2_g_pre[l]), wgu2, wd2, row(ffn2_g_post[l]))
    return h.reshape(batch, seq, D_MODEL)
```

```python
import functools

import numpy as np
import jax
import jax.numpy as jnp
from jax import lax
from jax.experimental import pallas as pl
from jax.experimental.pallas import tpu as pltpu

F32 = jnp.float32
BF16 = jnp.bfloat16

D_MODEL = 1024
FOX_HEADS = 8
FOX_HEAD_DIM = 64
GLA_HEADS = 4
GLA_HEAD_K = 64
GLA_HEAD_V = 128
GLA_GATE_RANK = 16
GLA_GATE_TAU = 16.0
GLA_CHUNK = 64
FOX_W = FOX_HEADS * FOX_HEAD_DIM
GLA_KW = GLA_HEADS * GLA_HEAD_K
GLA_VW = GLA_HEADS * GLA_HEAD_V
IN_SPLITS = (FOX_W, FOX_W, FOX_W, FOX_HEADS, GLA_KW, GLA_KW, GLA_VW, GLA_VW, GLA_GATE_RANK)
N_MEM = 256
MEM_HEADS = 4
MEM_HEAD_DIM = D_MODEL // MEM_HEADS
D_FF = 2816
MACARON_W = 0.5
RMS_EPS = 1e-6
LOG2E = 1.4426950408889634

LANES = 128
VMEM_LIMIT_BYTES = 56 * 1024 * 1024

TOKEN_TILE = 512
POST_TILE = 1024
POST_SPLIT = 4
FFN_TILE = 1024
FFN_SPLIT = 2
FFN_SIDE_ROWS = 64
FFN_CHUNK = 256
FOX_TQ = 256
FOX_TK = 256
FOX_HEADS_PER_STEP = 4
FOX_TILES_PER_STEP = 16
FOX_LAZY_HEADROOM = 60.0
FOX_SKIP_MARGIN = 160.0
FOX_NORM_SLACK = 1.02
FOX_VROWS = FOX_HEAD_DIM + 16
GLA_TILE = 512
GLA_CHUNKS_PER_STEP = 4

AUG_HI, AUG_MID, AUG_LO, AUG_ONE = 0, 8, 16, 24
GA_LANE = 8


def _rms_scale(x):
    return x * lax.rsqrt(jnp.mean(x * x, axis=-1, keepdims=True) + RMS_EPS)


def _sigmoid(x):
    return 1.0 / (1.0 + jnp.exp(-x))


def _log_sigmoid(x):
    return jnp.minimum(x, 0.0) - jnp.log(1.0 + jnp.exp(-jnp.abs(x)))


def _dot(a, b):
    return jnp.dot(a, b, preferred_element_type=F32)


def _dot_nt(a, b):
    return lax.dot_general(a, b, (((1,), (1,)), ((), ())), preferred_element_type=F32)


def _dot_tn(a, b):
    return lax.dot_general(a, b, (((0,), (0,)), ((), ())), preferred_element_type=F32)


def _split3(x):
    hi = x.astype(BF16)
    r1 = x - hi.astype(F32)
    mid = r1.astype(BF16)
    lo = (r1 - mid.astype(F32)).astype(BF16)
    return hi, mid, lo


def _const_spec(shape):
    zeros = (0,) * len(shape)
    return pl.BlockSpec(shape, lambda *_: zeros, pipeline_mode=pl.Buffered(1))


def _ffn_kernel(h_ref, gpre_ref, wgu_ref, wd_ref, gpost_ref, o_ref):
    n_chunks = D_FF // FFN_CHUNK
    rows = h_ref.shape[0] // FFN_SPLIT
    n_slices = min(n_chunks, rows // FFN_SIDE_ROWS)
    slices = [slice(s * (rows // n_slices), (s + 1) * (rows // n_slices)) for s in range(n_slices)]

    def shifted(sl, r):
        return slice(r * rows + sl.start, r * rows + sl.stop)

    def opening(sl):
        return (_rms_scale(h_ref[sl, :]) * gpre_ref[...]).astype(BF16)

    def closing(sl, acc_rows):
        o_ref[sl, :] = h_ref[sl, :] + MACARON_W * (_rms_scale(acc_rows) * gpost_ref[...])

    xn = opening(slice(0, rows))
    prev_acc = None
    for r in range(FFN_SPLIT):
        acc, nxt = None, []
        for j in range(n_chunks):
            gate = _dot(xn, wgu_ref[:, j * FFN_CHUNK:(j + 1) * FFN_CHUNK])
            up = _dot(xn, wgu_ref[:, D_FF + j * FFN_CHUNK:D_FF + (j + 1) * FFN_CHUNK])
            act = (gate * _sigmoid(gate) * up).astype(BF16)
            part = _dot(act, wd_ref[j * FFN_CHUNK:(j + 1) * FFN_CHUNK, :])
            acc = part if acc is None else acc + part
            if j < n_slices:
                if r + 1 < FFN_SPLIT:
                    nxt.append(opening(shifted(slices[j], r + 1)))
                if prev_acc is not None:
                    closing(shifted(slices[j], r - 1), prev_acc[slices[j], :])
        prev_acc = acc
        if nxt:
            xn = jnp.concatenate(nxt, axis=0)
    closing(slice((FFN_SPLIT - 1) * rows, FFN_SPLIT * rows), prev_acc)


def _prep_ffn(w_gu, w_down):
    return w_gu.astype(BF16), w_down.astype(BF16)


def _ffn(h, g_pre, wgu, wd, g_post):
    t = h.shape[0]
    tm = min(FFN_TILE, t)
    assert t % tm == 0
    row = pl.BlockSpec((tm, D_MODEL), lambda i: (i, 0))
    return pl.pallas_call(
        _ffn_kernel,
        out_shape=jax.ShapeDtypeStruct((t, D_MODEL), F32),
        grid=(t // tm,),
        in_specs=[row, _const_spec((1, D_MODEL)), _const_spec(wgu.shape), _const_spec(wd.shape),
                  _const_spec((1, D_MODEL))],
        out_specs=row,
        compiler_params=pltpu.CompilerParams(dimension_semantics=("parallel",),
                                             vmem_limit_bytes=VMEM_LIMIT_BYTES),
        name="ffn",
    )(h, g_pre, wgu, wd, g_post)


MIX_QW = FOX_HEADS * LANES
MIX_G0 = 2 * FOX_W
MIX_GW = 2 * GLA_KW + 2 * GLA_VW + LANES


def _prep_mix(w_in, w_a2, b_a, b_f):
    offs = np.cumsum(IN_SPLITS)[:-1].tolist()
    fq, fk, fv, fz, gq, gk, gv, gr, ga = jnp.split(w_in, offs, axis=1)
    pad = jnp.zeros((D_MODEL, LANES - FOX_HEADS - GLA_GATE_RANK), F32)
    w1 = jnp.concatenate([fq, fk, gq, gk, gv, gr, fz, ga, pad], axis=1).astype(BF16)
    wvt = fv.T.astype(BF16)
    wa2 = jnp.zeros((LANES, GLA_KW), F32).at[GA_LANE:GA_LANE + GLA_GATE_RANK].set(w_a2).astype(BF16)
    bfp = jnp.zeros((1, LANES), F32).at[0, :FOX_HEADS].set(b_f)
    return w1, wvt, wa2, bfp, b_a.reshape(1, GLA_KW)


def _aug_base(h):
    return FOX_HEAD_DIM if h % 2 == 0 else 0


def _aug_placement():
    p = np.zeros((LANES, 2 * MIX_QW), np.float32)
    for h in range(FOX_HEADS):
        qb = h * LANES + _aug_base(h)
        kb = MIX_QW + h * LANES + _aug_base(h)
        for i, src in enumerate((AUG_HI, AUG_MID, AUG_LO)):
            p[src + h, qb + i] = 1.0
            p[AUG_ONE, kb + i] = 1.0
            p[AUG_ONE, qb + 3 + i] = 1.0
            p[src + h, kb + 3 + i] = -1.0
    return jnp.asarray(p, BF16)


def _mix_in_kernel(h_ref, g_ref, w1_ref, wvt_ref, wa2_ref, bf_ref, ba_ref, p_ref, tri_ref, grp_ref, grpt_ref,
                   qaug_ref, kaug_ref, vt_ref, gq_ref, gk_ref, gv_ref, gr_ref, la_ref, own_ref, skip_ref,
                   carry_ref, hist_ref, *, fox_tk):
    @pl.when(pl.program_id(1) == 0)
    def _():
        carry_ref[...] = jnp.zeros_like(carry_ref)
        hist_ref[...] = jnp.full(hist_ref.shape, -jnp.inf, F32)

    tm = h_ref.shape[0]
    xn = (_rms_scale(h_ref[...]) * g_ref[...]).astype(BF16)
    zq = _dot(xn, w1_ref[:, :FOX_W]) * (FOX_HEAD_DIM ** -0.5 * LOG2E)
    zk = _dot(xn, w1_ref[:, FOX_W:MIX_G0])
    zg = _dot(xn, w1_ref[:, MIX_G0:])
    vt = _dot_nt(wvt_ref[...], xn).astype(BF16)
    for hd in range(FOX_HEADS):
        vt_ref[0, hd * FOX_VROWS:hd * FOX_VROWS + FOX_HEAD_DIM, :] = vt[hd * FOX_HEAD_DIM:(hd + 1) * FOX_HEAD_DIM]
        vt_ref[0, hd * FOX_VROWS + FOX_HEAD_DIM:(hd + 1) * FOX_VROWS, :] = jnp.ones(
            (FOX_VROWS - FOX_HEAD_DIM, tm), BF16)

    o = 0
    gq_ref[...] = zg[:, o:o + GLA_KW]; o += GLA_KW
    gk_ref[...] = zg[:, o:o + GLA_KW]; o += GLA_KW
    gv_ref[...] = zg[:, o:o + GLA_VW].astype(BF16); o += GLA_VW
    gr_ref[...] = zg[:, o:o + GLA_VW]; o += GLA_VW
    small = zg[:, o:o + LANES]

    xa = _dot(small.astype(BF16), wa2_ref[...]) + ba_ref[...]
    la_ref[...] = _log_sigmoid(xa) * (1.0 / GLA_GATE_TAU)

    lane = lax.broadcasted_iota(jnp.int32, (tm, LANES), 1)
    logf = jnp.where(lane < FOX_HEADS, _log_sigmoid(small + bf_ref[...]), 0.0)
    tri = tri_ref[...]
    l_hi, l_mid, l_lo = _split3(logf)
    l_all = (l_hi.astype(F32) + pltpu.roll(l_mid.astype(F32), AUG_MID, 1)
             + pltpu.roll(l_lo.astype(F32), AUG_LO, 1)).astype(BF16)
    part = _dot(tri, l_all)
    cum = part + pltpu.roll(part, LANES - AUG_MID, 1) + pltpu.roll(part, LANES - AUG_LO, 1)
    cum = jnp.where(lane < FOX_HEADS, cum, 0.0) + carry_ref[0:1, :]
    carry_ref[0:1, :] = cum[tm - 1:tm, :]

    c2 = cum * LOG2E
    own_ref[0] = _dot_nt(grpt_ref[...], (zq * zk).astype(BF16))[:FOX_HEADS]

    qn2 = _dot((zq * zq).astype(BF16), grp_ref[...])
    kn2 = _dot((zk * zk).astype(BF16), grp_ref[...])
    nsub = tm // fox_tk
    rows = []
    kn_run = carry_ref[1:2, :]
    for sb in range(nsub):
        r0 = sb * fox_tk
        kn_run = jnp.maximum(kn_run, jnp.sqrt(jnp.max(kn2[r0:r0 + fox_tk], axis=0, keepdims=True)))
        qn = jnp.sqrt(jnp.max(qn2[r0:r0 + fox_tk], axis=0, keepdims=True))
        reach = 2.0 * (FOX_NORM_SLACK * FOX_NORM_SLACK) * qn * kn_run + c2[r0:r0 + 1] + FOX_SKIP_MARGIN
        rows.append(jnp.sum(jnp.where(reach < hist_ref[...], 1.0, 0.0), axis=0, keepdims=True))
        hist_ref[pl.ds(pl.program_id(1) * nsub + sb, 1), :] = c2[r0 + fox_tk - 1:r0 + fox_tk]
    carry_ref[1:2, :] = kn_run
    skip_ref[0] = jnp.concatenate(rows + [jnp.zeros((8 - nsub, LANES), F32)], axis=0)

    c_hi, c_mid, c_lo = _split3(c2)
    packed = (c_hi.astype(F32) + pltpu.roll(c_mid.astype(F32), AUG_MID, 1)
              + pltpu.roll(c_lo.astype(F32), AUG_LO, 1))
    packed = jnp.where(lane == AUG_ONE, 1.0, packed)
    aug = _dot(packed.astype(BF16), p_ref[...])
    low_half = lane < FOX_HEAD_DIM
    for hd in range(FOX_HEADS):
        pair = slice((hd // 2) * LANES, (hd // 2 + 1) * LANES)
        cols = slice(hd * LANES, (hd + 1) * LANES)
        own = low_half if hd % 2 == 0 else jnp.logical_not(low_half)
        qaug_ref[:, cols] = (jnp.where(own, zq[:, pair], 0.0) + aug[:, cols]).astype(BF16)
        kaug_ref[:, cols] = (jnp.where(own, zk[:, pair], 0.0)
                             + aug[:, MIX_QW + hd * LANES:MIX_QW + (hd + 1) * LANES]).astype(BF16)


def _mix_in(h, g, w1, wvt, wa2, bfp, ba, batch, seq):
    tm = min(TOKEN_TILE, seq)
    nt = seq // tm
    tri = jnp.asarray(np.tril(np.ones((tm, tm), np.float32)), BF16)
    grp_np = np.arange(FOX_W)[:, None] // FOX_HEAD_DIM == np.arange(LANES)[None, :]
    grp = jnp.asarray(grp_np, BF16)
    grpt = jnp.asarray(grp_np[:, :16].T, BF16)
    fox_tk = min(FOX_TK, seq)
    row = lambda w: pl.BlockSpec((tm, w), lambda b, i: (b * nt + i, 0))
    t = batch * seq
    sds = jax.ShapeDtypeStruct
    return pl.pallas_call(
        functools.partial(_mix_in_kernel, fox_tk=fox_tk),
        out_shape=(sds((t, MIX_QW), BF16), sds((t, MIX_QW), BF16),
                   sds((batch, FOX_HEADS * FOX_VROWS, seq), BF16),
                   sds((t, GLA_KW), F32), sds((t, GLA_KW), F32), sds((t, GLA_VW), BF16),
                   sds((t, GLA_VW), F32), sds((t, GLA_KW), F32), sds((batch, FOX_HEADS, seq), F32),
                   sds((batch * nt, 8, LANES), F32)),
        grid=(batch, nt),
        in_specs=[row(D_MODEL), _const_spec((1, D_MODEL)), _const_spec(w1.shape), _const_spec(wvt.shape),
                  _const_spec(wa2.shape), _const_spec(bfp.shape), _const_spec(ba.shape),
                  _const_spec((LANES, 2 * MIX_QW)), _const_spec((tm, tm)), _const_spec((FOX_W, LANES)),
                  _const_spec((16, FOX_W))],
        out_specs=(row(MIX_QW), row(MIX_QW),
                   pl.BlockSpec((1, FOX_HEADS * FOX_VROWS, tm), lambda b, i: (b, 0, i)),
                   row(GLA_KW), row(GLA_KW), row(GLA_VW), row(GLA_VW), row(GLA_KW),
                   pl.BlockSpec((1, FOX_HEADS, tm), lambda b, i: (b, 0, i)),
                   pl.BlockSpec((1, 8, LANES), lambda b, i: (b * nt + i, 0, 0))),
        scratch_shapes=[pltpu.VMEM((2, LANES), F32), pltpu.VMEM((seq // fox_tk, LANES), F32)],
        compiler_params=pltpu.CompilerParams(dimension_semantics=("arbitrary", "arbitrary"),
                                             vmem_limit_bytes=VMEM_LIMIT_BYTES),
        name="mix_in",
    )(h, g, w1, wvt, wa2, bfp, ba, _aug_placement(), tri, grp, grpt)


def _fox_kernel(*refs, tiles, **params):
    def tile(sub, carry):
        _fox_tile(sub, *refs, tiles=tiles, **params)
        return carry

    lax.fori_loop(0, tiles, tile, 0)


def _fox_tile(sub, skip_ref, q_ref, k_ref, vt_ref, own_ref, mask_ref, mask2_ref, o_ref, p_ref, *, tq, tk, heads,
              nt, tiles):
    qi = pl.program_id(2) * tiles + sub
    qrows = pl.ds(pl.multiple_of(sub * tq, tq), tq)
    dh = FOX_HEAD_DIM
    vr = FOX_VROWS
    assert tq == tk

    def scores(hh, ks):
        cols = slice(hh * LANES, (hh + 1) * LANES)
        return _dot_nt(k_ref[pl.ds(ks, tk), cols], q_ref[qrows, cols])

    def exact_step(hh, j, carry, masked):
        m, l, acc = carry
        ks = pl.multiple_of(j * tk, tk)
        s = scores(hh, ks)
        if masked:
            s = s + mask_ref[...]
        m_new = jnp.maximum(m, jnp.max(s, axis=0, keepdims=True))
        p = jnp.exp2(s - m_new)
        alpha = jnp.exp2(m - m_new)
        l = alpha * l + jnp.sum(p, axis=0, keepdims=True)
        vt = vt_ref[0, hh * vr:hh * vr + dh, pl.ds(ks, tk)]
        acc = alpha * acc + _dot(vt, p.astype(BF16))
        return m_new, l, acc

    def weighted_values(hh, ks):
        return _dot(vt_ref[0, hh * vr:(hh + 1) * vr, pl.ds(ks, tk)], p_ref[hh])

    def lazy_block(ks, ks_prev, carry, masked):
        out = []
        for hh in range(heads):
            m, acc, over, shift_prev = carry[hh]
            s = scores(hh, ks)
            acc = (acc + weighted_values(hh, ks_prev)) * jnp.exp2(-shift_prev)
            p = jnp.exp2((s + mask_ref[...] if masked else s) - m).astype(BF16)
            p_ref[hh] = p
            over_b = jnp.log2(jnp.max(p, axis=0, keepdims=True).astype(F32))
            shift = jnp.maximum(over_b, 0.0)
            out.append((m + shift, acc, jnp.maximum(over, over_b), shift))
        return tuple(out)

    def finish(hh, l, acc):
        o_ref[0, hh * dh:(hh + 1) * dh, qrows] = (acc / l).astype(BF16)

    def redo():
        init = (jnp.full((1, tq), -jnp.inf, F32), jnp.zeros((1, tq), F32), jnp.zeros((dh, tq), F32))
        for hh in range(heads):
            c = exact_step(hh, qi, init, True)
            c = lax.fori_loop(0, qi, lambda j, c, hh=hh: exact_step(hh, j, c, False), c)
            finish(hh, c[1], c[2])

    kd = pl.multiple_of(qi * tk, tk)
    head0 = pl.program_id(1) * heads
    own = [own_ref[0, pl.ds(head0 + hh, 1), qrows] for hh in range(heads)]
    skip0 = (pl.program_id(0) * FOX_HEADS + head0) * nt + qi
    j0 = skip_ref[skip0]
    for hh in range(1, heads):
        j0 = jnp.minimum(j0, skip_ref[skip0 + hh * nt])
    one_before = jnp.logical_and(qi >= 1, j0 == qi - 1)

    @pl.when(one_before)
    def _():
        ks = pl.multiple_of(kd - tk, tk)
        cols = [slice(hh * LANES, (hh + 1) * LANES) for hh in range(heads)]
        s = [_dot_nt(k_ref[pl.ds(ks, 2 * tk), c], q_ref[qrows, c]) for c in cols]
        p = [jnp.exp2(x + mask2_ref[...] - m).astype(BF16) for x, m in zip(s, own)]
        acc = [_dot(vt_ref[0, hh * vr:(hh + 1) * vr, pl.ds(ks, 2 * tk)], p[hh]) for hh in range(heads)]
        over = [jnp.log2(jnp.max(x, axis=0, keepdims=True).astype(F32)) for x in p]
        for hh in range(heads):
            finish(hh, acc[hh][dh:dh + 1], acc[hh][:dh])

        @pl.when(jnp.logical_not(jnp.max(jnp.concatenate(over, axis=0)) <= FOX_LAZY_HEADROOM))
        def _():
            redo()

    @pl.when(jnp.logical_not(one_before))
    def _():
        zero = jnp.zeros((1, tq), F32)
        carry = tuple((own[hh], jnp.zeros((vr, tq), F32), zero, zero) for hh in range(heads))
        p_ref[...] = jnp.zeros_like(p_ref)

        def blocks(j, n, c):
            for i in range(n):
                prev = jnp.maximum(j + i - 1, 0)
                c = lazy_block(pl.multiple_of((j + i) * tk, tk), pl.multiple_of(prev * tk, tk), c, False)
            return c

        odd = (qi - j0) % 2
        carry = lax.fori_loop(0, odd, lambda j, c: blocks(j0 + j, 1, c), carry)
        carry = lax.fori_loop(0, (qi - j0) // 2, lambda t, c: blocks(j0 + odd + 2 * t, 2, c), carry)
        carry = lazy_block(kd, pl.multiple_of(jnp.maximum(qi - 1, 0) * tk, tk), carry, True)
        for hh in range(heads):
            acc = carry[hh][1] + weighted_values(hh, kd)
            finish(hh, acc[dh:dh + 1], acc[:dh])

        @pl.when(jnp.logical_not(jnp.max(jnp.concatenate([c[2] for c in carry], axis=0))
                                 <= FOX_LAZY_HEADROOM))
        def _():
            redo()


def _fox(qaug, kaug, vt, own, skip, batch, seq):
    tq = min(FOX_TQ, seq)
    tk = min(FOX_TK, tq)
    nq = seq // tq
    hp = FOX_HEADS_PER_STEP
    tiles = min(FOX_TILES_PER_STEP, nq)
    tqs = tq * tiles
    assert nq % tiles == 0
    mask_np = np.where(np.arange(tk)[:, None] <= np.arange(tq)[None, :], 0.0, -np.inf)
    mask = jnp.asarray(mask_np, F32)
    mask2 = jnp.asarray(np.concatenate([np.zeros((tk, tq)), mask_np]), F32)
    nsub = min(TOKEN_TILE, seq) // tk
    table = skip[:, :nsub, :FOX_HEADS].reshape(batch, nq, FOX_HEADS).transpose(0, 2, 1)
    table = table.astype(jnp.int32).reshape(-1)
    tile_map = lambda f: (lambda b, h, i, tbl: f(b, h, i))
    return pl.pallas_call(
        functools.partial(_fox_kernel, tq=tq, tk=tk, heads=hp, nt=nq, tiles=tiles),
        out_shape=jax.ShapeDtypeStruct((batch, FOX_W, seq), BF16),
        grid_spec=pltpu.PrefetchScalarGridSpec(
            num_scalar_prefetch=1,
            grid=(batch, FOX_HEADS // hp, nq // tiles),
            in_specs=[pl.BlockSpec((tqs, hp * LANES), tile_map(lambda b, h, i: (b * (nq // tiles) + i, h))),
                      pl.BlockSpec((seq, hp * LANES), tile_map(lambda b, h, i: (b, h))),
                      pl.BlockSpec((1, hp * FOX_VROWS, seq), tile_map(lambda b, h, i: (b, h, 0))),
                      pl.BlockSpec((1, FOX_HEADS, tqs), tile_map(lambda b, h, i: (b, 0, i))),
                      pl.BlockSpec((tk, tq), tile_map(lambda b, h, i: (0, 0)),
                                   pipeline_mode=pl.Buffered(1)),
                      pl.BlockSpec((2 * tk, tq), tile_map(lambda b, h, i: (0, 0)),
                                   pipeline_mode=pl.Buffered(1))],
            out_specs=pl.BlockSpec((1, hp * FOX_HEAD_DIM, tqs), tile_map(lambda b, h, i: (b, h, i))),
            scratch_shapes=[pltpu.VMEM((hp, tk, tq), BF16)]),
        compiler_params=pltpu.CompilerParams(dimension_semantics=("parallel", "parallel", "arbitrary"),
                                             vmem_limit_bytes=VMEM_LIMIT_BYTES),
        name="fox",
    )(table, qaug, kaug, vt, own, mask, mask2)


def _gla_kernel(gq_ref, gk_ref, gv_ref, gr_ref, la_ref, g_ref, tri_ref, o_ref, st_ref, *, nb, tile):
    @pl.when(pl.program_id(0) == 0)
    def _():
        st_ref[...] = jnp.zeros_like(st_ref)

    c = GLA_CHUNK
    lane_head = lax.broadcasted_iota(jnp.int32, (c, GLA_KW), 1) // GLA_HEAD_K
    st_head = lax.broadcasted_iota(jnp.int32, (GLA_HEAD_V, GLA_KW), 1) // GLA_HEAD_K
    causal_stack = (lax.broadcasted_iota(jnp.int32, (GLA_HEADS * c, c), 0) % c
                    >= lax.broadcasted_iota(jnp.int32, (GLA_HEADS * c, c), 1))
    tri = tri_ref[...]

    def chunks(ci, _):
        n = GLA_CHUNKS_PER_STEP
        rows = [pl.ds(pl.multiple_of((ci * n + i) * c, c), c) for i in range(n)]
        ib = [(i, b) for i in range(n) for b in range(nb)]
        ibh = [(i, b, h) for i, b in ib for h in range(GLA_HEADS)]
        vcols = lambda h: slice(h * GLA_HEAD_V, (h + 1) * GLA_HEAD_V)
        bc = {}
        for i, b in ib:
            la = la_ref[b, rows[i], :]
            la_hi = la.astype(BF16)
            la_lo = (la - la_hi.astype(F32)).astype(BF16)
            bc[i, b] = _dot(tri, la_hi) + _dot(tri, la_lo)
        qd, kd, ke, decay = {}, {}, {}, {}
        for i, b in ib:
            bl = bc[i, b][c - 1:c, :]
            k = gk_ref[b, rows[i], :]
            qd[i, b] = gq_ref[b, rows[i], :] * (GLA_HEAD_K ** -0.5) * jnp.exp(bc[i, b])
            kd[i, b] = (k * jnp.exp(-bc[i, b])).astype(BF16)
            ke[i, b] = (k * jnp.exp(bl - bc[i, b])).astype(BF16)
            decay[i, b] = jnp.exp(bl)
        hrows = lambda h: slice(h * c, (h + 1) * c)
        qstack = {(i, b): jnp.concatenate(
            [jnp.where(lane_head == h, qd[i, b], 0.0).astype(BF16) for h in range(GLA_HEADS)], axis=0)
            for i, b in ib}
        a = {(i, b): jnp.where(causal_stack, _dot_nt(qstack[i, b], kd[i, b]), 0.0).astype(BF16)
             for i, b in ib}
        vh = {(i, b, h): gv_ref[b, rows[i], vcols(h)] for i, b, h in ibh}
        o_intra = {(i, b, h): _dot(a[i, b][hrows(h), :], vh[i, b, h]) for i, b, h in ibh}
        u_all = {(i, b): _dot_tn(gv_ref[b, rows[i], :], ke[i, b]) for i, b in ib}
        u = {(i, b, h): u_all[i, b][vcols(h), :] for i, b, h in ibh}
        st = [st_ref[b] for b in range(nb)]
        for i in range(n):
            o_inter = [_dot_nt(qstack[i, b], st[b].astype(BF16)) for b in range(nb)]
            for b in range(nb):
                for h in range(GLA_HEADS):
                    val = o_intra[i, b, h] + o_inter[b][hrows(h), :]
                    gr = gr_ref[b, rows[i], vcols(h)]
                    o_ref[b, rows[i], vcols(h)] = (_rms_scale(val) * g_ref[:, vcols(h)]
                                                   * (gr * _sigmoid(gr))).astype(BF16)
            st = [st[b] * decay[i, b] + sum(jnp.where(st_head == h, u[i, b, h], 0.0)
                                             for h in range(GLA_HEADS)) for b in range(nb)]
        for b in range(nb):
            st_ref[b] = st[b]
        return 0

    lax.fori_loop(0, tile // (c * GLA_CHUNKS_PER_STEP), chunks, 0)


def _gla(gq, gk, gv, gr, la, g_gla, batch, seq):
    tile = min(GLA_TILE, seq)
    tri = jnp.asarray(np.tril(np.ones((GLA_CHUNK, GLA_CHUNK), np.float32)), BF16)
    blk = lambda w: pl.BlockSpec((batch, tile, w), lambda i: (0, i, 0))
    r3 = lambda x: x.reshape(batch, seq, x.shape[-1])
    out = pl.pallas_call(
        functools.partial(_gla_kernel, nb=batch, tile=tile),
        out_shape=jax.ShapeDtypeStruct((batch, seq, GLA_VW), BF16),
        grid=(seq // tile,),
        in_specs=[blk(GLA_KW), blk(GLA_KW), blk(GLA_VW), blk(GLA_VW), blk(GLA_KW),
                  _const_spec((1, GLA_VW)), _const_spec((GLA_CHUNK, GLA_CHUNK))],
        out_specs=blk(GLA_VW),
        scratch_shapes=[pltpu.VMEM((batch, GLA_HEAD_V, GLA_KW), F32)],
        compiler_params=pltpu.CompilerParams(dimension_semantics=("arbitrary",),
                                             vmem_limit_bytes=VMEM_LIMIT_BYTES),
        name="gla",
    )(r3(gq), r3(gk), r3(gv), r3(gr), r3(la), g_gla, tri)
    return out.reshape(batch * seq, GLA_VW)


def _memkv_kernel(mem_ref, g_ref, w_ref, k_ref, v_ref):
    mn = (_rms_scale(mem_ref[0]) * g_ref[...]).astype(BF16)
    kv = _dot(mn, w_ref[...])
    k_ref[0] = kv[:, :D_MODEL].astype(BF16)
    v_ref[0] = kv[:, D_MODEL:].astype(BF16)


def _memkv(mem, g, w):
    batch = mem.shape[0]
    blk = pl.BlockSpec((1, N_MEM, D_MODEL), lambda b: (b, 0, 0))
    sds = jax.ShapeDtypeStruct((batch, N_MEM, D_MODEL), BF16)
    return pl.pallas_call(
        _memkv_kernel,
        out_shape=(sds, sds),
        grid=(batch,),
        in_specs=[blk, _const_spec((1, D_MODEL)), _const_spec(w.shape)],
        out_specs=(blk, blk),
        compiler_params=pltpu.CompilerParams(dimension_semantics=("parallel",),
                                             vmem_limit_bytes=VMEM_LIMIT_BYTES),
        name="memkv",
    )(mem, g, w)


def _post_kernel(h_ref, oft_ref, og_ref, wof_ref, wog_ref, g1_ref, g2_ref, wq_ref, km_ref, vm_ref,
                 wo_ref, g3_ref, out_ref):
    rows = h_ref.shape[0] // POST_SPLIT
    groups = [slice(r * rows, (r + 1) * rows) for r in range(POST_SPLIT)]
    cols = [slice(hd * MEM_HEAD_DIM, (hd + 1) * MEM_HEAD_DIM) for hd in range(MEM_HEADS)]
    m = [_dot_tn(oft_ref[0, :, g], wof_ref[...]) + _dot(og_ref[g, :], wog_ref[...]) for g in groups]
    h2 = [h_ref[g, :] + _rms_scale(x) * g1_ref[...] for g, x in zip(groups, m)]
    u = [(_rms_scale(x) * g2_ref[...]).astype(BF16) for x in h2]
    q = [_dot(x, wq_ref[...]) * (MEM_HEAD_DIM ** -0.5) for x in u]
    s = [[_dot_nt(x[:, c].astype(BF16), km_ref[0, :, c]) for c in cols] for x in q]
    p = [[jnp.exp(y - jnp.max(y, axis=-1, keepdims=True)) for y in x] for x in s]
    p = [[(y / jnp.sum(y, axis=-1, keepdims=True)).astype(BF16) for y in x] for x in p]
    o = [jnp.concatenate([_dot(y, vm_ref[0, :, c]) for y, c in zip(x, cols)], axis=1).astype(BF16) for x in p]
    c = [_dot(x, wo_ref[...]) for x in o]
    for g, x, y in zip(groups, h2, c):
        out_ref[g, :] = x + _rms_scale(y) * g3_ref[...]


def _post(h, oft, og, wof, wog, g1, g2, wq, km, vm, wo, g3, batch, seq):
    tm = min(POST_TILE, seq)
    nt = seq // tm
    row = lambda w: pl.BlockSpec((tm, w), lambda b, i: (b * nt + i, 0))
    memblk = pl.BlockSpec((1, N_MEM, D_MODEL), lambda b, i: (b, 0, 0))
    g = _const_spec((1, D_MODEL))
    return pl.pallas_call(
        _post_kernel,
        out_shape=jax.ShapeDtypeStruct((batch * seq, D_MODEL), F32),
        grid=(batch, nt),
        in_specs=[row(D_MODEL), pl.BlockSpec((1, FOX_W, tm), lambda b, i: (b, 0, i)), row(GLA_VW),
                  _const_spec(wof.shape), _const_spec(wog.shape), g, g, _const_spec(wq.shape),
                  memblk, memblk, _const_spec(wo.shape), g],
        out_specs=row(D_MODEL),
        compiler_params=pltpu.CompilerParams(dimension_semantics=("parallel", "parallel"),
                                             vmem_limit_bytes=VMEM_LIMIT_BYTES),
        name="post",
    )(h, oft, og, wof, wog, g1, g2, wq, km, vm, wo, g3)


def kernel(x, mem, ffn1_g_pre, ffn1_w_gu, ffn1_w_down, ffn1_g_post, mix_g_pre, mix_w_in, mix_w_a2, mix_b_a,
           mix_b_f, mix_g_gla, mix_w_out, mix_g_post, mem_g_pre, mem_g_kv, mem_w_q, mem_w_kv, mem_w_o,
           mem_g_post, ffn2_g_pre, ffn2_w_gu, ffn2_w_down, ffn2_g_post):
    batch, seq, _ = x.shape
    depth = ffn1_g_pre.shape[0]
    row = lambda v: v.reshape(1, -1)
    h = x.reshape(batch * seq, D_MODEL)
    for l in range(depth):
        wgu1, wd1 = _prep_ffn(ffn1_w_gu[l], ffn1_w_down[l])
        wgu2, wd2 = _prep_ffn(ffn2_w_gu[l], ffn2_w_down[l])
        w1, wvt, wa2, bfp, ba = _prep_mix(mix_w_in[l], mix_w_a2[l], mix_b_a[l], mix_b_f[l])
        w_out = mix_w_out[l].astype(BF16)

        h = _ffn(h, row(ffn1_g_pre[l]), wgu1, wd1, row(ffn1_g_post[l]))
        qaug, kaug, vt, gq, gk, gv, gr, la, own, skip = _mix_in(h, row(mix_g_pre[l]), w1, wvt, wa2, bfp, ba,
                                                                batch, seq)
        oft = _fox(qaug, kaug, vt, own, skip, batch, seq)
        og = _gla(gq, gk, gv, gr, la, row(mix_g_gla[l]), batch, seq)
        km, vm = _memkv(mem, row(mem_g_kv[l]), mem_w_kv[l].astype(BF16))
        h = _post(h, oft, og, w_out[:FOX_W], w_out[FOX_W:], row(mix_g_post[l]), row(mem_g_pre[l]),
                  mem_w_q[l].astype(BF16), km, vm, mem_w_o[l].astype(BF16), row(mem_g_post[l]), batch, seq)
        h = _ffn(h, row(ffn2_g_pre[l]), wgu2, wd2, row(ffn2_g_post[l]))
    return h.reshape(batch, seq, D_MODEL)
```

```python
import functools

import numpy as np
import jax
import jax.numpy as jnp
from jax import lax
from jax.experimental import pallas as pl
from jax.experimental.pallas import tpu as pltpu

F32 = jnp.float32
BF16 = jnp.bfloat16

D_MODEL = 1024
FOX_HEADS = 8
FOX_HEAD_DIM = 64
GLA_HEADS = 4
GLA_HEAD_K = 64
GLA_HEAD_V = 128
GLA_GATE_RANK = 16
GLA_GATE_TAU = 16.0
GLA_CHUNK = 64
FOX_W = FOX_HEADS * FOX_HEAD_DIM
GLA_KW = GLA_HEADS * GLA_HEAD_K
GLA_VW = GLA_HEADS * GLA_HEAD_V
IN_SPLITS = (FOX_W, FOX_W, FOX_W, FOX_HEADS, GLA_KW, GLA_KW, GLA_VW, GLA_VW, GLA_GATE_RANK)
N_MEM = 256
MEM_HEADS = 4
MEM_HEAD_DIM = D_MODEL // MEM_HEADS
D_FF = 2816
MACARON_W = 0.5
RMS_EPS = 1e-6
LOG2E = 1.4426950408889634

LANES = 128
VMEM_LIMIT_BYTES = 56 * 1024 * 1024

TOKEN_TILE = 512
POST_TILE = 1024
POST_SPLIT = 4
FFN_TILE = 1024
FFN_SPLIT = 2
FFN_SIDE_ROWS = 64
FFN_CHUNK = 256
FOX_TQ = 256
FOX_TK = 256
FOX_HEADS_PER_STEP = 4
FOX_TILES_PER_STEP = 16
FOX_LAZY_HEADROOM = 60.0
FOX_SKIP_MARGIN = 160.0
FOX_NORM_SLACK = 1.02
FOX_VROWS = FOX_HEAD_DIM + 16
GLA_TILE = 512
GLA_CHUNKS_PER_STEP = 4

AUG_HI, AUG_MID, AUG_LO, AUG_ONE = 0, 8, 16, 24
GA_LANE = 8


def _rms_scale(x):
    return x * lax.rsqrt(jnp.mean(x * x, axis=-1, keepdims=True) + RMS_EPS)


def _sigmoid(x):
    return 1.0 / (1.0 + jnp.exp(-x))


def _log_sigmoid(x):
    return jnp.minimum(x, 0.0) - jnp.log(1.0 + jnp.exp(-jnp.abs(x)))


def _dot(a, b):
    return jnp.dot(a, b, preferred_element_type=F32)


def _dot_nt(a, b):
    return lax.dot_general(a, b, (((1,), (1,)), ((), ())), preferred_element_type=F32)


def _dot_tn(a, b):
    return lax.dot_general(a, b, (((0,), (0,)), ((), ())), preferred_element_type=F32)


def _split3(x):
    hi = x.astype(BF16)
    r1 = x - hi.astype(F32)
    mid = r1.astype(BF16)
    lo = (r1 - mid.astype(F32)).astype(BF16)
    return hi, mid, lo


def _const_spec(shape):
    zeros = (0,) * len(shape)
    return pl.BlockSpec(shape, lambda *_: zeros, pipeline_mode=pl.Buffered(1))


def _ffn_kernel(h_ref, gpre_ref, wgu_ref, wd_ref, gpost_ref, o_ref):
    n_chunks = D_FF // FFN_CHUNK
    rows = h_ref.shape[0] // FFN_SPLIT
    n_slices = min(n_chunks, rows // FFN_SIDE_ROWS)
    slices = [slice(s * (rows // n_slices), (s + 1) * (rows // n_slices)) for s in range(n_slices)]

    def shifted(sl, r):
        return slice(r * rows + sl.start, r * rows + sl.stop)

    def opening(sl):
        return (_rms_scale(h_ref[sl, :]) * gpre_ref[...]).astype(BF16)

    def closing(sl, acc_rows):
        o_ref[sl, :] = h_ref[sl, :] + MACARON_W * (_rms_scale(acc_rows) * gpost_ref[...])

    xn = opening(slice(0, rows))
    prev_acc = None
    for r in range(FFN_SPLIT):
        acc, nxt = None, []
        for j in range(n_chunks):
            gate = _dot(xn, wgu_ref[:, j * FFN_CHUNK:(j + 1) * FFN_CHUNK])
            up = _dot(xn, wgu_ref[:, D_FF + j * FFN_CHUNK:D_FF + (j + 1) * FFN_CHUNK])
            act = (gate * _sigmoid(gate) * up).astype(BF16)
            part = _dot(act, wd_ref[j * FFN_CHUNK:(j + 1) * FFN_CHUNK, :])
            acc = part if acc is None else acc + part
            if j < n_slices:
                if r + 1 < FFN_SPLIT:
                    nxt.append(opening(shifted(slices[j], r + 1)))
                if prev_acc is not None:
                    closing(shifted(slices[j], r - 1), prev_acc[slices[j], :])
        prev_acc = acc
        if nxt:
            xn = jnp.concatenate(nxt, axis=0)
    closing(slice((FFN_SPLIT - 1) * rows, FFN_SPLIT * rows), prev_acc)


def _prep_ffn(w_gu, w_down):
    return w_gu.astype(BF16), w_down.astype(BF16)


def _ffn(h, g_pre, wgu, wd, g_post):
    t = h.shape[0]
    tm = min(FFN_TILE, t)
    assert t % tm == 0
    row = pl.BlockSpec((tm, D_MODEL), lambda i: (i, 0))
    return pl.pallas_call(
        _ffn_kernel,
        out_shape=jax.ShapeDtypeStruct((t, D_MODEL), F32),
        grid=(t // tm,),
        in_specs=[row, _const_spec((1, D_MODEL)), _const_spec(wgu.shape), _const_spec(wd.shape),
                  _const_spec((1, D_MODEL))],
        out_specs=row,
        compiler_params=pltpu.CompilerParams(dimension_semantics=("parallel",),
                                             vmem_limit_bytes=VMEM_LIMIT_BYTES),
        name="ffn",
    )(h, g_pre, wgu, wd, g_post)


MIX_QW = FOX_HEADS * LANES
MIX_G0 = 2 * FOX_W
MIX_GW = 2 * GLA_KW + 2 * GLA_VW + LANES


def _prep_mix(w_in, w_a2, b_a, b_f):
    offs = np.cumsum(IN_SPLITS)[:-1].tolist()
    fq, fk, fv, fz, gq, gk, gv, gr, ga = jnp.split(w_in, offs, axis=1)
    pad = jnp.zeros((D_MODEL, LANES - FOX_HEADS - GLA_GATE_RANK), F32)
    w1 = jnp.concatenate([fq, fk, gq, gk, gv, gr, fz, ga, pad], axis=1).astype(BF16)
    wvt = fv.T.astype(BF16)
    wa2 = jnp.zeros((LANES, GLA_KW), F32).at[GA_LANE:GA_LANE + GLA_GATE_RANK].set(w_a2).astype(BF16)
    bfp = jnp.zeros((1, LANES), F32).at[0, :FOX_HEADS].set(b_f)
    return w1, wvt, wa2, bfp, b_a.reshape(1, GLA_KW)


def _aug_base(h):
    return FOX_HEAD_DIM if h % 2 == 0 else 0


def _aug_placement():
    p = np.zeros((LANES, 2 * MIX_QW), np.float32)
    for h in range(FOX_HEADS):
        qb = h * LANES + _aug_base(h)
        kb = MIX_QW + h * LANES + _aug_base(h)
        for i, src in enumerate((AUG_HI, AUG_MID, AUG_LO)):
            p[src + h, qb + i] = 1.0
            p[AUG_ONE, kb + i] = 1.0
            p[AUG_ONE, qb + 3 + i] = 1.0
            p[src + h, kb + 3 + i] = -1.0
    return jnp.asarray(p, BF16)


def _mix_in_kernel(h_ref, g_ref, w1_ref, wvt_ref, wa2_ref, bf_ref, ba_ref, p_ref, tri_ref, grp_ref, grpt_ref,
                   qaug_ref, kaug_ref, vt_ref, gq_ref, gk_ref, gv_ref, gr_ref, la_ref, own_ref, skip_ref,
                   carry_ref, hist_ref, *, fox_tk):
    @pl.when(pl.program_id(1) == 0)
    def _():
        carry_ref[...] = jnp.zeros_like(carry_ref)
        hist_ref[...] = jnp.full(hist_ref.shape, -jnp.inf, F32)

    tm = h_ref.shape[0]
    xn = (_rms_scale(h_ref[...]) * g_ref[...]).astype(BF16)
    zq = _dot(xn, w1_ref[:, :FOX_W]) * (FOX_HEAD_DIM ** -0.5 * LOG2E)
    zk = _dot(xn, w1_ref[:, FOX_W:MIX_G0])
    zg = _dot(xn, w1_ref[:, MIX_G0:])
    vt = _dot_nt(wvt_ref[...], xn).astype(BF16)
    for hd in range(FOX_HEADS):
        vt_ref[0, hd * FOX_VROWS:hd * FOX_VROWS + FOX_HEAD_DIM, :] = vt[hd * FOX_HEAD_DIM:(hd + 1) * FOX_HEAD_DIM]
        vt_ref[0, hd * FOX_VROWS + FOX_HEAD_DIM:(hd + 1) * FOX_VROWS, :] = jnp.ones(
            (FOX_VROWS - FOX_HEAD_DIM, tm), BF16)

    o = 0
    gq_ref[...] = zg[:, o:o + GLA_KW]; o += GLA_KW
    gk_ref[...] = zg[:, o:o + GLA_KW]; o += GLA_KW
    gv_ref[...] = zg[:, o:o + GLA_VW].astype(BF16); o += GLA_VW
    gr_ref[...] = zg[:, o:o + GLA_VW]; o += GLA_VW
    small = zg[:, o:o + LANES]

    xa = _dot(small.astype(BF16), wa2_ref[...]) + ba_ref[...]
    la_ref[...] = _log_sigmoid(xa) * (1.0 / GLA_GATE_TAU)

    lane = lax.broadcasted_iota(jnp.int32, (tm, LANES), 1)
    logf = jnp.where(lane < FOX_HEADS, _log_sigmoid(small + bf_ref[...]), 0.0)
    tri = tri_ref[...]
    l_hi, l_mid, l_lo = _split3(logf)
    l_all = (l_hi.astype(F32) + pltpu.roll(l_mid.astype(F32), AUG_MID, 1)
             + pltpu.roll(l_lo.astype(F32), AUG_LO, 1)).astype(BF16)
    part = _dot(tri, l_all)
    cum = part + pltpu.roll(part, LANES - AUG_MID, 1) + pltpu.roll(part, LANES - AUG_LO, 1)
    cum = jnp.where(lane < FOX_HEADS, cum, 0.0) + carry_ref[0:1, :]
    carry_ref[0:1, :] = cum[tm - 1:tm, :]

    c2 = cum * LOG2E
    own_ref[0] = _dot_nt(grpt_ref[...], (zq * zk).astype(BF16))[:FOX_HEADS]

    qn2 = _dot((zq * zq).astype(BF16), grp_ref[...])
    kn2 = _dot((zk * zk).astype(BF16), grp_ref[...])
    nsub = tm // fox_tk
    rows = []
    kn_run = carry_ref[1:2, :]
    for sb in range(nsub):
        r0 = sb * fox_tk
        kn_run = jnp.maximum(kn_run, jnp.sqrt(jnp.max(kn2[r0:r0 + fox_tk], axis=0, keepdims=True)))
        qn = jnp.sqrt(jnp.max(qn2[r0:r0 + fox_tk], axis=0, keepdims=True))
        reach = 2.0 * (FOX_NORM_SLACK * FOX_NORM_SLACK) * qn * kn_run + c2[r0:r0 + 1] + FOX_SKIP_MARGIN
        rows.append(jnp.sum(jnp.where(reach < hist_ref[...], 1.0, 0.0), axis=0, keepdims=True))
        hist_ref[pl.ds(pl.program_id(1) * nsub + sb, 1), :] = c2[r0 + fox_tk - 1:r0 + fox_tk]
    carry_ref[1:2, :] = kn_run
    skip_ref[0] = jnp.concatenate(rows + [jnp.zeros((8 - nsub, LANES), F32)], axis=0)

    c_hi, c_mid, c_lo = _split3(c2)
    packed = (c_hi.astype(F32) + pltpu.roll(c_mid.astype(F32), AUG_MID, 1)
              + pltpu.roll(c_lo.astype(F32), AUG_LO, 1))
    packed = jnp.where(lane == AUG_ONE, 1.0, packed)
    aug = _dot(packed.astype(BF16), p_ref[...])
    low_half = lane < FOX_HEAD_DIM
    for hd in range(FOX_HEADS):
        pair = slice((hd // 2) * LANES, (hd // 2 + 1) * LANES)
        cols = slice(hd * LANES, (hd + 1) * LANES)
        own = low_half if hd % 2 == 0 else jnp.logical_not(low_half)
        qaug_ref[:, cols] = (jnp.where(own, zq[:, pair], 0.0) + aug[:, cols]).astype(BF16)
        kaug_ref[:, cols] = (jnp.where(own, zk[:, pair], 0.0)
                             + aug[:, MIX_QW + hd * LANES:MIX_QW + (hd + 1) * LANES]).astype(BF16)


def _mix_in(h, g, w1, wvt, wa2, bfp, ba, batch, seq):
    tm = min(TOKEN_TILE, seq)
    nt = seq // tm
    tri = jnp.asarray(np.tril(np.ones((tm, tm), np.float32)), BF16)
    grp_np = np.arange(FOX_W)[:, None] // FOX_HEAD_DIM == np.arange(LANES)[None, :]
    grp = jnp.asarray(grp_np, BF16)
    grpt = jnp.asarray(grp_np[:, :16].T, BF16)
    fox_tk = min(FOX_TK, seq)
    row = lambda w: pl.BlockSpec((tm, w), lambda b, i: (b * nt + i, 0))
    t = batch * seq
    sds = jax.ShapeDtypeStruct
    return pl.pallas_call(
        functools.partial(_mix_in_kernel, fox_tk=fox_tk),
        out_shape=(sds((t, MIX_QW), BF16), sds((t, MIX_QW), BF16),
                   sds((batch, FOX_HEADS * FOX_VROWS, seq), BF16),
                   sds((t, GLA_KW), F32), sds((t, GLA_KW), F32), sds((t, GLA_VW), BF16),
                   sds((t, GLA_VW), F32), sds((t, GLA_KW), F32), sds((batch, FOX_HEADS, seq), F32),
                   sds((batch * nt, 8, LANES), F32)),
        grid=(batch, nt),
        in_specs=[row(D_MODEL), _const_spec((1, D_MODEL)), _const_spec(w1.shape), _const_spec(wvt.shape),
                  _const_spec(wa2.shape), _const_spec(bfp.shape), _const_spec(ba.shape),
                  _const_spec((LANES, 2 * MIX_QW)), _const_spec((tm, tm)), _const_spec((FOX_W, LANES)),
                  _const_spec((16, FOX_W))],
        out_specs=(row(MIX_QW), row(MIX_QW),
                   pl.BlockSpec((1, FOX_HEADS * FOX_VROWS, tm), lambda b, i: (b, 0, i)),
                   row(GLA_KW), row(GLA_KW), row(GLA_VW), row(GLA_VW), row(GLA_KW),
                   pl.BlockSpec((1, FOX_HEADS, tm), lambda b, i: (b, 0, i)),
                   pl.BlockSpec((1, 8, LANES), lambda b, i: (b * nt + i, 0, 0))),
        scratch_shapes=[pltpu.VMEM((2, LANES), F32), pltpu.VMEM((seq // fox_tk, LANES), F32)],
        compiler_params=pltpu.CompilerParams(dimension_semantics=("arbitrary", "arbitrary"),
                                             vmem_limit_bytes=VMEM_LIMIT_BYTES),
        name="mix_in",
    )(h, g, w1, wvt, wa2, bfp, ba, _aug_placement(), tri, grp, grpt)


def _fox_kernel(*refs, tiles, **params):
    def tile(sub, carry):
        _fox_tile(sub, *refs, tiles=tiles, **params)
        return carry

    lax.fori_loop(0, tiles, tile, 0)


def _fox_tile(sub, skip_ref, q_ref, k_ref, vt_ref, own_ref, mask_ref, mask2_ref, o_ref, p_ref, *, tq, tk, heads,
              nt, tiles):
    qi = pl.program_id(2) * tiles + sub
    qrows = pl.ds(pl.multiple_of(sub * tq, tq), tq)
    dh = FOX_HEAD_DIM
    vr = FOX_VROWS
    assert tq == tk

    def scores(hh, ks):
        cols = slice(hh * LANES, (hh + 1) * LANES)
        return _dot_nt(k_ref[pl.ds(ks, tk), cols], q_ref[qrows, cols])

    def exact_step(hh, j, carry, masked):
        m, l, acc = carry
        ks = pl.multiple_of(j * tk, tk)
        s = scores(hh, ks)
        if masked:
            s = s + mask_ref[...]
        m_new = jnp.maximum(m, jnp.max(s, axis=0, keepdims=True))
        p = jnp.exp2(s - m_new)
        alpha = jnp.exp2(m - m_new)
        l = alpha * l + jnp.sum(p, axis=0, keepdims=True)
        vt = vt_ref[0, hh * vr:hh * vr + dh, pl.ds(ks, tk)]
        acc = alpha * acc + _dot(vt, p.astype(BF16))
        return m_new, l, acc

    def weighted_values(hh, ks):
        return _dot(vt_ref[0, hh * vr:(hh + 1) * vr, pl.ds(ks, tk)], p_ref[hh])

    def lazy_block(ks, ks_prev, carry, masked):
        out = []
        for hh in range(heads):
            m, acc, over, shift_prev = carry[hh]
            s = scores(hh, ks)
            acc = (acc + weighted_values(hh, ks_prev)) * jnp.exp2(-shift_prev)
            p = jnp.exp2((s + mask_ref[...] if masked else s) - m).astype(BF16)
            p_ref[hh] = p
            over_b = jnp.log2(jnp.max(p, axis=0, keepdims=True).astype(F32))
            shift = jnp.maximum(over_b, 0.0)
            out.append((m + shift, acc, jnp.maximum(over, over_b), shift))
        return tuple(out)

    def finish(hh, l, acc):
        o_ref[0, hh * dh:(hh + 1) * dh, qrows] = (acc / l).astype(BF16)

    def redo():
        init = (jnp.full((1, tq), -jnp.inf, F32), jnp.zeros((1, tq), F32), jnp.zeros((dh, tq), F32))
        for hh in range(heads):
            c = exact_step(hh, qi, init, True)
            c = lax.fori_loop(0, qi, lambda j, c, hh=hh: exact_step(hh, j, c, False), c)
            finish(hh, c[1], c[2])

    kd = pl.multiple_of(qi * tk, tk)
    head0 = pl.program_id(1) * heads
    own = [own_ref[0, pl.ds(head0 + hh, 1), qrows] for hh in range(heads)]
    skip0 = (pl.program_id(0) * FOX_HEADS + head0) * nt + qi
    j0 = skip_ref[skip0]
    for hh in range(1, heads):
        j0 = jnp.minimum(j0, skip_ref[skip0 + hh * nt])
    one_before = jnp.logical_and(qi >= 1, j0 == qi - 1)

    @pl.when(one_before)
    def _():
        ks = pl.multiple_of(kd - tk, tk)
        cols = [slice(hh * LANES, (hh + 1) * LANES) for hh in range(heads)]
        s = [_dot_nt(k_ref[pl.ds(ks, 2 * tk), c], q_ref[qrows, c]) for c in cols]
        p = [jnp.exp2(x + mask2_ref[...] - m).astype(BF16) for x, m in zip(s, own)]
        acc = [_dot(vt_ref[0, hh * vr:(hh + 1) * vr, pl.ds(ks, 2 * tk)], p[hh]) for hh in range(heads)]
        over = [jnp.log2(jnp.max(x, axis=0, keepdims=True).astype(F32)) for x in p]
        for hh in range(heads):
            finish(hh, acc[hh][dh:dh + 1], acc[hh][:dh])

        @pl.when(jnp.logical_not(jnp.max(jnp.concatenate(over, axis=0)) <= FOX_LAZY_HEADROOM))
        def _():
            redo()

    @pl.when(jnp.logical_not(one_before))
    def _():
        zero = jnp.zeros((1, tq), F32)
        carry = tuple((own[hh], jnp.zeros((vr, tq), F32), zero, zero) for hh in range(heads))
        p_ref[...] = jnp.zeros_like(p_ref)

        def blocks(j, n, c):
            for i in range(n):
                prev = jnp.maximum(j + i - 1, 0)
                c = lazy_block(pl.multiple_of((j + i) * tk, tk), pl.multiple_of(prev * tk, tk), c, False)
            return c

        odd = (qi - j0) % 2
        carry = lax.fori_loop(0, odd, lambda j, c: blocks(j0 + j, 1, c), carry)
        carry = lax.fori_loop(0, (qi - j0) // 2, lambda t, c: blocks(j0 + odd + 2 * t, 2, c), carry)
        carry = lazy_block(kd, pl.multiple_of(jnp.maximum(qi - 1, 0) * tk, tk), carry, True)
        for hh in range(heads):
            acc = carry[hh][1] + weighted_values(hh, kd)
            finish(hh, acc[dh:dh + 1], acc[:dh])

        @pl.when(jnp.logical_not(jnp.max(jnp.concatenate([c[2] for c in carry], axis=0))
                                 <= FOX_LAZY_HEADROOM))
        def _():
            redo()


def _fox(qaug, kaug, vt, own, skip, batch, seq):
    tq = min(FOX_TQ, seq)
    tk = min(FOX_TK, tq)
    nq = seq // tq
    hp = FOX_HEADS_PER_STEP
    tiles = min(FOX_TILES_PER_STEP, nq)
    tqs = tq * tiles
    assert nq % tiles == 0
    mask_np = np.where(np.arange(tk)[:, None] <= np.arange(tq)[None, :], 0.0, -np.inf)
    mask = jnp.asarray(mask_np, F32)
    mask2 = jnp.asarray(np.concatenate([np.zeros((tk, tq)), mask_np]), F32)
    nsub = min(TOKEN_TILE, seq) // tk
    table = skip[:, :nsub, :FOX_HEADS].reshape(batch, nq, FOX_HEADS).transpose(0, 2, 1)
    table = table.astype(jnp.int32).reshape(-1)
    tile_map = lambda f: (lambda b, h, i, tbl: f(b, h, i))
    return pl.pallas_call(
        functools.partial(_fox_kernel, tq=tq, tk=tk, heads=hp, nt=nq, tiles=tiles),
        out_shape=jax.ShapeDtypeStruct((batch, FOX_W, seq), BF16),
        grid_spec=pltpu.PrefetchScalarGridSpec(
            num_scalar_prefetch=1,
            grid=(batch, FOX_HEADS // hp, nq // tiles),
            in_specs=[pl.BlockSpec((tqs, hp * LANES), tile_map(lambda b, h, i: (b * (nq // tiles) + i, h))),
                      pl.BlockSpec((seq, hp * LANES), tile_map(lambda b, h, i: (b, h))),
                      pl.BlockSpec((1, hp * FOX_VROWS, seq), tile_map(lambda b, h, i: (b, h, 0))),
                      pl.BlockSpec((1, FOX_HEADS, tqs), tile_map(lambda b, h, i: (b, 0, i))),
                      pl.BlockSpec((tk, tq), tile_map(lambda b, h, i: (0, 0)),
                                   pipeline_mode=pl.Buffered(1)),
                      pl.BlockSpec((2 * tk, tq), tile_map(lambda b, h, i: (0, 0)),
                                   pipeline_mode=pl.Buffered(1))],
            out_specs=pl.BlockSpec((1, hp * FOX_HEAD_DIM, tqs), tile_map(lambda b, h, i: (b, h, i))),
            scratch_shapes=[pltpu.VMEM((hp, tk, tq), BF16)]),
        compiler_params=pltpu.CompilerParams(dimension_semantics=("parallel", "parallel", "arbitrary"),
                                             vmem_limit_bytes=VMEM_LIMIT_BYTES),
        name="fox",
    )(table, qaug, kaug, vt, own, mask, mask2)


def _gla_kernel(gq_ref, gk_ref, gv_ref, gr_ref, la_ref, g_ref, tri_ref, o_ref, st_ref, *, nb, tile):
    @pl.when(pl.program_id(0) == 0)
    def _():
        st_ref[...] = jnp.zeros_like(st_ref)

    c = GLA_CHUNK
    lane_head = lax.broadcasted_iota(jnp.int32, (c, GLA_KW), 1) // GLA_HEAD_K
    st_head = lax.broadcasted_iota(jnp.int32, (GLA_HEAD_V, GLA_KW), 1) // GLA_HEAD_K
    causal_stack = (lax.broadcasted_iota(jnp.int32, (GLA_HEADS * c, c), 0) % c
                    >= lax.broadcasted_iota(jnp.int32, (GLA_HEADS * c, c), 1))
    tri = tri_ref[...]

    def chunks(ci, _):
        n = GLA_CHUNKS_PER_STEP
        rows = [pl.ds(pl.multiple_of((ci * n + i) * c, c), c) for i in range(n)]
        ib = [(i, b) for i in range(n) for b in range(nb)]
        ibh = [(i, b, h) for i, b in ib for h in range(GLA_HEADS)]
        vcols = lambda h: slice(h * GLA_HEAD_V, (h + 1) * GLA_HEAD_V)
        bc = {}
        for i, b in ib:
            la = la_ref[b, rows[i], :]
            la_hi = la.astype(BF16)
            la_lo = (la - la_hi.astype(F32)).astype(BF16)
            bc[i, b] = _dot(tri, la_hi) + _dot(tri, la_lo)
        qd, kd, ke, decay = {}, {}, {}, {}
        for i, b in ib:
            bl = bc[i, b][c - 1:c, :]
            k = gk_ref[b, rows[i], :]
            qd[i, b] = gq_ref[b, rows[i], :] * (GLA_HEAD_K ** -0.5) * jnp.exp(bc[i, b])
            kd[i, b] = (k * jnp.exp(-bc[i, b])).astype(BF16)
            ke[i, b] = (k * jnp.exp(bl - bc[i, b])).astype(BF16)
            decay[i, b] = jnp.exp(bl)
        hrows = lambda h: slice(h * c, (h + 1) * c)
        qstack = {(i, b): jnp.concatenate(
            [jnp.where(lane_head == h, qd[i, b], 0.0).astype(BF16) for h in range(GLA_HEADS)], axis=0)
            for i, b in ib}
        a = {(i, b): jnp.where(causal_stack, _dot_nt(qstack[i, b], kd[i, b]), 0.0).astype(BF16)
             for i, b in ib}
        vh = {(i, b, h): gv_ref[b, rows[i], vcols(h)] for i, b, h in ibh}
        o_intra = {(i, b, h): _dot(a[i, b][hrows(h), :], vh[i, b, h]) for i, b, h in ibh}
        u_all = {(i, b): _dot_tn(gv_ref[b, rows[i], :], ke[i, b]) for i, b in ib}
        u = {(i, b, h): u_all[i, b][vcols(h), :] for i, b, h in ibh}
        st = [st_ref[b] for b in range(nb)]
        for i in range(n):
            o_inter = [_dot_nt(qstack[i, b], st[b].astype(BF16)) for b in range(nb)]
            for b in range(nb):
                for h in range(GLA_HEADS):
                    val = o_intra[i, b, h] + o_inter[b][hrows(h), :]
                    gr = gr_ref[b, rows[i], vcols(h)]
                    o_ref[b, rows[i], vcols(h)] = (_rms_scale(val) * g_ref[:, vcols(h)]
                                                   * (gr * _sigmoid(gr))).astype(BF16)
            st = [st[b] * decay[i, b] + sum(jnp.where(st_head == h, u[i, b, h], 0.0)
                                             for h in range(GLA_HEADS)) for b in range(nb)]
        for b in range(nb):
            st_ref[b] = st[b]
        return 0

    lax.fori_loop(0, tile // (c * GLA_CHUNKS_PER_STEP), chunks, 0)


def _gla(gq, gk, gv, gr, la, g_gla, batch, seq):
    tile = min(GLA_TILE, seq)
    tri = jnp.asarray(np.tril(np.ones((GLA_CHUNK, GLA_CHUNK), np.float32)), BF16)
    blk = lambda w: pl.BlockSpec((batch, tile, w), lambda i: (0, i, 0))
    r3 = lambda x: x.reshape(batch, seq, x.shape[-1])
    out = pl.pallas_call(
        functools.partial(_gla_kernel, nb=batch, tile=tile),
        out_shape=jax.ShapeDtypeStruct((batch, seq, GLA_VW), BF16),
        grid=(seq // tile,),
        in_specs=[blk(GLA_KW), blk(GLA_KW), blk(GLA_VW), blk(GLA_VW), blk(GLA_KW),
                  _const_spec((1, GLA_VW)), _const_spec((GLA_CHUNK, GLA_CHUNK))],
        out_specs=blk(GLA_VW),
        scratch_shapes=[pltpu.VMEM((batch, GLA_HEAD_V, GLA_KW), F32)],
        compiler_params=pltpu.CompilerParams(dimension_semantics=("arbitrary",),
                                             vmem_limit_bytes=VMEM_LIMIT_BYTES),
        name="gla",
    )(r3(gq), r3(gk), r3(gv), r3(gr), r3(la), g_gla, tri)
    return out.reshape(batch * seq, GLA_VW)


def _memkv_kernel(mem_ref, g_ref, w_ref, k_ref, v_ref):
    mn = (_rms_scale(mem_ref[0]) * g_ref[...]).astype(BF16)
    kv = _dot(mn, w_ref[...])
    k_ref[0] = kv[:, :D_MODEL].T.astype(BF16)
    v_ref[0] = kv[:, D_MODEL:].astype(BF16)


def _memkv(mem, g, w):
    batch = mem.shape[0]
    blk = pl.BlockSpec((1, N_MEM, D_MODEL), lambda b: (b, 0, 0))
    blk_t = pl.BlockSpec((1, D_MODEL, N_MEM), lambda b: (b, 0, 0))
    sds = jax.ShapeDtypeStruct((batch, N_MEM, D_MODEL), BF16)
    sds_t = jax.ShapeDtypeStruct((batch, D_MODEL, N_MEM), BF16)
    return pl.pallas_call(
        _memkv_kernel,
        out_shape=(sds_t, sds),
        grid=(batch,),
        in_specs=[blk, _const_spec((1, D_MODEL)), _const_spec(w.shape)],
        out_specs=(blk_t, blk),
        compiler_params=pltpu.CompilerParams(dimension_semantics=("parallel",),
                                             vmem_limit_bytes=VMEM_LIMIT_BYTES),
        name="memkv",
    )(mem, g, w)


def _post_kernel(h_ref, oft_ref, og_ref, wof_ref, wog_ref, g1_ref, g2_ref, wq_ref, km_ref, vm_ref,
                 wo_ref, g3_ref, out_ref):
    rows = h_ref.shape[0] // POST_SPLIT
    groups = [slice(r * rows, (r + 1) * rows) for r in range(POST_SPLIT)]
    cols = [slice(hd * MEM_HEAD_DIM, (hd + 1) * MEM_HEAD_DIM) for hd in range(MEM_HEADS)]
    m = [_dot_tn(oft_ref[0, :, g], wof_ref[...]) + _dot(og_ref[g, :], wog_ref[...]) for g in groups]
    h2 = [h_ref[g, :] + _rms_scale(x) * g1_ref[...] for g, x in zip(groups, m)]
    u = [(_rms_scale(x) * g2_ref[...]).astype(BF16) for x in h2]
    q = [_dot(x, wq_ref[...]) * (MEM_HEAD_DIM ** -0.5) for x in u]
    s = [[_dot(x[:, c].astype(BF16), km_ref[0, c, :]) for c in cols] for x in q]
    p = [[jnp.exp(y - jnp.max(y, axis=-1, keepdims=True)) for y in x] for x in s]
    p = [[(y / jnp.sum(y, axis=-1, keepdims=True)).astype(BF16) for y in x] for x in p]
    o = [jnp.concatenate([_dot(y, vm_ref[0, :, c]) for y, c in zip(x, cols)], axis=1).astype(BF16) for x in p]
    c = [_dot(x, wo_ref[...]) for x in o]
    for g, x, y in zip(groups, h2, c):
        out_ref[g, :] = x + _rms_scale(y) * g3_ref[...]


def _post(h, oft, og, wof, wog, g1, g2, wq, km, vm, wo, g3, batch, seq):
    tm = min(POST_TILE, seq)
    nt = seq // tm
    row = lambda w: pl.BlockSpec((tm, w), lambda b, i: (b * nt + i, 0))
    memblk = pl.BlockSpec((1, N_MEM, D_MODEL), lambda b, i: (b, 0, 0))
    g = _const_spec((1, D_MODEL))
    return pl.pallas_call(
        _post_kernel,
        out_shape=jax.ShapeDtypeStruct((batch * seq, D_MODEL), F32),
        grid=(batch, nt),
        in_specs=[row(D_MODEL), pl.BlockSpec((1, FOX_W, tm), lambda b, i: (b, 0, i)), row(GLA_VW),
                  _const_spec(wof.shape), _const_spec(wog.shape), g, g, _const_spec(wq.shape),
                  pl.BlockSpec((1, D_MODEL, N_MEM), lambda b, i: (b, 0, 0)), memblk, _const_spec(wo.shape), g],
        out_specs=row(D_MODEL),
        compiler_params=pltpu.CompilerParams(dimension_semantics=("parallel", "parallel"),
                                             vmem_limit_bytes=VMEM_LIMIT_BYTES),
        name="post",
    )(h, oft, og, wof, wog, g1, g2, wq, km, vm, wo, g3)


def kernel(x, mem, ffn1_g_pre, ffn1_w_gu, ffn1_w_down, ffn1_g_post, mix_g_pre, mix_w_in, mix_w_a2, mix_b_a,
           mix_b_f, mix_g_gla, mix_w_out, mix_g_post, mem_g_pre, mem_g_kv, mem_w_q, mem_w_kv, mem_w_o,
           mem_g_post, ffn2_g_pre, ffn2_w_gu, ffn2_w_down, ffn2_g_post):
    batch, seq, _ = x.shape
    depth = ffn1_g_pre.shape[0]
    row = lambda v: v.reshape(1, -1)
    h = x.reshape(batch * seq, D_MODEL)
    for l in range(depth):
        wgu1, wd1 = _prep_ffn(ffn1_w_gu[l], ffn1_w_down[l])
        wgu2, wd2 = _prep_ffn(ffn2_w_gu[l], ffn2_w_down[l])
        w1, wvt, wa2, bfp, ba = _prep_mix(mix_w_in[l], mix_w_a2[l], mix_b_a[l], mix_b_f[l])
        w_out = mix_w_out[l].astype(BF16)

        h = _ffn(h, row(ffn1_g_pre[l]), wgu1, wd1, row(ffn1_g_post[l]))
        qaug, kaug, vt, gq, gk, gv, gr, la, own, skip = _mix_in(h, row(mix_g_pre[l]), w1, wvt, wa2, bfp, ba,
                                                                batch, seq)
        oft = _fox(qaug, kaug, vt, own, skip, batch, seq)
        og = _gla(gq, gk, gv, gr, la, row(mix_g_gla[l]), batch, seq)
        km, vm = _memkv(mem, row(mem_g_kv[l]), mem_w_kv[l].astype(BF16))
        h = _post(h, oft, og, w_out[:FOX_W], w_out[FOX_W:], row(mix_g_post[l]), row(mem_g_pre[l]),
                  mem_w_q[l].astype(BF16), km, vm, mem_w_o[l].astype(BF16), row(mem_g_post[l]), batch, seq)
        h = _ffn(h, row(ffn2_g_pre[l]), wgu2, wd2, row(ffn2_g_post[l]))
    return h.reshape(batch, seq, D_MODEL)
```
